```python
import jax, jax.numpy as jnp
from jax import lax
import numpy as np

D_MODEL = 1024
BATCH = 1
SEQ = 16384
DEPTH = 1
DEC_BATCH = 128
DEC_SEQ = 1
PAST_LEN = 8192
PAGE_SIZE = 128

GROUPS = ((128, 1), (512, 4), (2048, 16))
N_GROUPS = len(GROUPS)
HEADS_PER_GROUP = 4
N_HEADS = N_GROUPS * HEADS_PER_GROUP
HEAD_DIM = 64
ATTN_WIDTH = N_HEADS * HEAD_DIM
ATTN_OUT_WIDTH = HEADS_PER_GROUP * HEAD_DIM
WINDOW_MAX = 2048
Q_BLOCK = 128
N_BUCKETS = 32
MAX_EXACT = 16
REL_MAX_DIST = 2048
CONV_WIDTH = D_MODEL // 2
CONV_KERNEL = 31
IN_COLS = 3 * ATTN_WIDTH + ATTN_OUT_WIDTH + 3 * CONV_WIDTH + 2 * D_MODEL
EPS = 1e-6
NEG = -1e30

kernel_name = 'hybrid_dilated_attn_conformer_step'


def _rmsnorm(x, g):
    x32 = x.astype(jnp.float32)
    y = x32 * lax.rsqrt(jnp.mean(x32 * x32, axis=-1, keepdims=True) + EPS)
    return (y * g.astype(jnp.float32)).astype(x.dtype)


def _layernorm(x, g, b):
    x32 = x.astype(jnp.float32)
    mu = jnp.mean(x32, axis=-1, keepdims=True)
    var = jnp.mean(jnp.square(x32 - mu), axis=-1, keepdims=True)
    y = (x32 - mu) * lax.rsqrt(var + EPS)
    return (y * g.astype(jnp.float32) + b.astype(jnp.float32)).astype(x.dtype)


def _t5_buckets(dist):
    n = np.asarray(dist, dtype=np.int64)
    large = MAX_EXACT + (np.log(np.maximum(n, 1) / MAX_EXACT) / np.log(REL_MAX_DIST / MAX_EXACT)
                         * (N_BUCKETS - MAX_EXACT)).astype(np.int64)
    large = np.minimum(large, N_BUCKETS - 1)
    return np.where(n < MAX_EXACT, n, large).astype(np.int32)


def _group_biases(rel_bias):
    out = []
    for g, (window, dil) in enumerate(GROUPS):
        buckets = _t5_buckets(dil * np.arange(window // dil + 1))
        tab = jnp.take(rel_bias, jnp.asarray(buckets), axis=0)
        out.append(tab[:, g * HEADS_PER_GROUP:(g + 1) * HEADS_PER_GROUP].T)
    return out


def _attend_block(q_blk, ks, vs, rows, biases):
    outs, lses = [], []
    scale = HEAD_DIM ** -0.5
    for g, (window, dil) in enumerate(GROUPS):
        n_keys = window // dil + 1
        key_rows = rows[:, None] - dil * jnp.arange(n_keys, dtype=jnp.int32)[None, :]
        valid = key_rows >= 0
        idx = jnp.maximum(key_rows, 0)
        kg = jnp.take(ks[g], idx, axis=1)
        vg = jnp.take(vs[g], idx, axis=1)
        qg = q_blk[:, :, g * HEADS_PER_GROUP:(g + 1) * HEADS_PER_GROUP]
        s = jnp.einsum('nthd,ntjhd->nthj', qg, kg).astype(jnp.float32) * scale
        s = s + biases[g].astype(jnp.float32)[None, None]
        s = jnp.where(valid[None, :, None, :], s, NEG)
        m = jnp.max(s, axis=-1, keepdims=True)
        p = jnp.exp(s - m)
        l = jnp.sum(p, axis=-1, keepdims=True)
        o = jnp.einsum('nthj,ntjhd->nthd', (p / l).astype(vg.dtype), vg)
        outs.append(o.astype(jnp.float32))
        lses.append((m + jnp.log(l))[..., 0])
    w = jax.nn.softmax(jnp.stack(lses), axis=0)
    o = jnp.einsum('gnth,gnthd->nthd', w, jnp.stack(outs))
    return o.reshape(o.shape[0], o.shape[1], ATTN_OUT_WIDTH).astype(q_blk.dtype)


def _dilated_mixture(q, k_seq, v_seq, p0, rel_bias):
    n, t = q.shape[0], q.shape[1]
    ks = [k_seq[:, :, g * HEADS_PER_GROUP:(g + 1) * HEADS_PER_GROUP] for g in range(N_GROUPS)]
    vs = [v_seq[:, :, g * HEADS_PER_GROUP:(g + 1) * HEADS_PER_GROUP] for g in range(N_GROUPS)]
    biases = _group_biases(rel_bias)
    qb = Q_BLOCK if t % Q_BLOCK == 0 else t
    nb = t // qb
    q_blocks = q.reshape(n, nb, qb, N_HEADS, HEAD_DIM).swapaxes(0, 1)

    def block(args):
        q_blk, b = args
        rows = p0 + b * qb + jnp.arange(qb, dtype=jnp.int32)
        return _attend_block(q_blk, ks, vs, rows, biases)

    o = lax.map(block, (q_blocks, jnp.arange(nb, dtype=jnp.int32)))
    return o.swapaxes(0, 1).reshape(n, t, ATTN_OUT_WIDTH)


def _layer(x, k_prefix, v_prefix, u_prefix, keep, rel_bias, norm_pre, w_in, w_out_a,
           conv_w, conv_b, ln_g, ln_b, w_out_b, w_o, norm_post):
    n, t, _ = x.shape
    h = _rmsnorm(x, norm_pre)
    proj = jnp.einsum('ntd,de->nte', h, w_in)
    cuts = [int(c) for c in np.cumsum([ATTN_WIDTH, ATTN_WIDTH, ATTN_WIDTH, ATTN_OUT_WIDTH,
                                       2 * CONV_WIDTH, CONV_WIDTH, D_MODEL])]
    q, k, v, g_a, glu, g_b, m_a, m_b = jnp.split(proj, cuts, axis=-1)
    q = q.reshape(n, t, N_HEADS, HEAD_DIM)
    k = k.reshape(n, t, N_HEADS, HEAD_DIM)
    v = v.reshape(n, t, N_HEADS, HEAD_DIM)
    k_seq = jnp.concatenate([k_prefix.astype(k.dtype), k], axis=1)
    v_seq = jnp.concatenate([v_prefix.astype(v.dtype), v], axis=1)
    a = _dilated_mixture(q, k_seq, v_seq, k_prefix.shape[1], rel_bias)
    y_a = jnp.einsum('nte,ed->ntd', a * jax.nn.silu(g_a), w_out_a)
    u = glu[..., :CONV_WIDTH] * jax.nn.sigmoid(glu[..., CONV_WIDTH:])
    u_ext = jnp.concatenate([u_prefix.astype(u.dtype), u], axis=1)
    c = lax.conv_general_dilated(u_ext, conv_w[:, None, :].astype(u.dtype), (1,), 'VALID',
                                 dimension_numbers=('NWC', 'WIO', 'NWC'),
                                 feature_group_count=CONV_WIDTH) + conv_b
    c = jax.nn.silu(_layernorm(c, ln_g, ln_b)) * jax.nn.silu(g_b)
    y_b = jnp.einsum('ntc,cd->ntd', c, w_out_b)
    merged = jax.nn.sigmoid(m_a) * y_a + jax.nn.sigmoid(m_b) * y_b
    out = _rmsnorm(jnp.einsum('ntd,de->nte', merged, w_o), norm_post)
    y = x + out
    L = k_seq.shape[1]
    return (y, k_seq[:, L - keep:], v_seq[:, L - keep:], u_ext[:, u_ext.shape[1] - (CONV_KERNEL - 1):])


def setup_inputs(seed: int = 0) -> dict:
    key = jax.random.key(seed)
    ks = jax.random.split(key, 18)
    w_buf = min(WINDOW_MAX, PAST_LEN)
    f = jnp.float32
    nrm = lambda k, s: jax.random.normal(k, s, dtype=f)
    return {
        'x_prompt': nrm(ks[0], (BATCH, SEQ, D_MODEL)),
        'x_sample': nrm(ks[1], (DEC_BATCH, DEC_SEQ, D_MODEL)),
        'cache_k': nrm(ks[2], (DEPTH, DEC_BATCH, w_buf, N_HEADS, HEAD_DIM)),
        'cache_v': nrm(ks[3], (DEPTH, DEC_BATCH, w_buf, N_HEADS, HEAD_DIM)),
        'state_conv': 0.7 * nrm(ks[4], (DEPTH, DEC_BATCH, CONV_KERNEL - 1, CONV_WIDTH)),
        'rel_bias': 0.5 * nrm(ks[5], (N_BUCKETS, N_HEADS)),
        'norm_pre': 1.0 + 0.05 * nrm(ks[6], (DEPTH, D_MODEL)),
        'w_in': nrm(ks[7], (DEPTH, D_MODEL, IN_COLS)) * D_MODEL ** -0.5,
        'w_out_a': nrm(ks[8], (DEPTH, ATTN_OUT_WIDTH, D_MODEL)) * ATTN_OUT_WIDTH ** -0.5,
        'conv_w': nrm(ks[9], (DEPTH, CONV_KERNEL, CONV_WIDTH)) * CONV_KERNEL ** -0.5,
        'conv_b': 0.01 * nrm(ks[10], (DEPTH, CONV_WIDTH)),
        'ln_g': 1.0 + 0.05 * nrm(ks[11], (DEPTH, CONV_WIDTH)),
        'ln_b': 0.01 * nrm(ks[12], (DEPTH, CONV_WIDTH)),
        'w_out_b': nrm(ks[13], (DEPTH, CONV_WIDTH, D_MODEL)) * CONV_WIDTH ** -0.5,
        'w_o': nrm(ks[14], (DEPTH, D_MODEL, D_MODEL)) * D_MODEL ** -0.5,
        'norm_post': 1.0 + 0.05 * nrm(ks[15], (DEPTH, D_MODEL)),
    }


def reference(x_prompt, x_sample, cache_k, cache_v, state_conv, rel_bias, norm_pre, w_in,
              w_out_a, conv_w, conv_b, ln_g, ln_b, w_out_b, w_o, norm_post):
    yp, ys = x_prompt, x_sample
    nb, seq = x_prompt.shape[0], x_prompt.shape[1]
    kp_l, vp_l, up_l, ks_l, vs_l, us_l = [], [], [], [], [], []
    for l in range(DEPTH):
        w = (rel_bias, norm_pre[l], w_in[l], w_out_a[l], conv_w[l], conv_b[l], ln_g[l], ln_b[l],
             w_out_b[l], w_o[l], norm_post[l])
        k0 = jnp.zeros((nb, 0, N_HEADS, HEAD_DIM), x_prompt.dtype)
        u0 = jnp.zeros((nb, CONV_KERNEL - 1, CONV_WIDTH), x_prompt.dtype)
        yp, kp, vp, up = _layer(yp, k0, k0, u0, min(WINDOW_MAX, seq), *w)
        ys, ksm, vsm, usm = _layer(ys, cache_k[l], cache_v[l], state_conv[l], cache_k.shape[2], *w)
        kp_l.append(kp); vp_l.append(vp); up_l.append(up)
        ks_l.append(ksm); vs_l.append(vsm); us_l.append(usm)
    return (yp, ys, jnp.stack(kp_l), jnp.stack(vp_l), jnp.stack(up_l),
            jnp.stack(ks_l), jnp.stack(vs_l), jnp.stack(us_l))
```

```python
import functools

import numpy as np
import jax
import jax.numpy as jnp
from jax import lax
from jax.experimental import pallas as pl
from jax.experimental.pallas import tpu as pltpu

D_MODEL = 1024
GROUPS = ((128, 1), (512, 4), (2048, 16))
N_GROUPS = len(GROUPS)
HEADS_PER_GROUP = 4
N_HEADS = N_GROUPS * HEADS_PER_GROUP
HEAD_DIM = 64
GROUP_WIDTH = HEADS_PER_GROUP * HEAD_DIM
ATTN_WIDTH = N_HEADS * HEAD_DIM
KEYS_PER_QUERY = 128
N_BUCKETS = 32
MAX_EXACT = 16
REL_MAX_DIST = 2048
CONV_WIDTH = D_MODEL // 2
CONV_KERNEL = 31
EPS = 1e-6
NEG = -1e30
SCALE = HEAD_DIM ** -0.5

C_QKV = 3 * ATTN_WIDTH
C_GA = C_QKV + GROUP_WIDTH
C_GLU = C_GA + 2 * CONV_WIDTH
C_GB = C_GLU + CONV_WIDTH
C_MA = C_GB + D_MODEL
C_MB = C_MA + D_MODEL

ROW_TILE = 512
Q_BLOCK = 128
LANES = 128
CARRY_ROWS = 32
VMEM_LIMIT = 52 * 1024 * 1024

f32 = jnp.float32
bf16 = jnp.bfloat16


def _rms(x, g):
    return x * lax.rsqrt(jnp.mean(x * x, axis=-1, keepdims=True) + EPS) * g


def _sigmoid(x):
    return 1.0 / (1.0 + jnp.exp(-x))


def _silu(x):
    return x * _sigmoid(x)


def _layernorm(c, g, b):
    mu = jnp.mean(c, axis=-1, keepdims=True)
    var = jnp.mean(jnp.square(c - mu), axis=-1, keepdims=True)
    return (c - mu) * lax.rsqrt(var + EPS) * g + b


def _mm(a, w):
    return jnp.dot(a.astype(bf16), w, preferred_element_type=f32)


def _conv_gate(c, gb, lg, lb, wob, mbp):
    c = _silu(_layernorm(c, lg, lb)) * _silu(gb)
    return _sigmoid(mbp) * _mm(c, wob)


def _tail(a, gas, gm, mb, x, woa, wo, npost):
    y_a = _mm(a * gas, woa)
    merged = gm * y_a + mb
    return x + _rms(_mm(merged, wo), npost)


def _combine(os, ls):
    m = jnp.maximum(jnp.maximum(ls[0], ls[1]), ls[2])
    es = [jnp.exp(l - m) for l in ls]
    den = es[0] + es[1] + es[2]
    return (es[0] * os[0] + es[1] * os[1] + es[2] * os[2]) / den


def _prompt_in_body(x_ref, np_ref, w_ref, cw_ref, cb_ref, lg_ref, lb_ref, wob_ref,
                    q0, k0, v0, q1, k1, v1, q2, k2, v2, klast, vlast, gas, gm, mb, ulast,
                    qkv_s, ubuf):
    i = pl.program_id(0)
    tm = x_ref.shape[0]
    h = _rms(x_ref[...], np_ref[...]).astype(bf16)

    qkv = jnp.dot(h, w_ref[:, 0:C_QKV], preferred_element_type=f32)
    klast[...] = qkv[:, ATTN_WIDTH:2 * ATTN_WIDTH]
    vlast[...] = qkv[:, 2 * ATTN_WIDTH:3 * ATTN_WIDTH]
    for j in range(C_QKV // LANES):
        qkv_s[j] = qkv[:, j * LANES:(j + 1) * LANES]
    outs = ((q0, k0, v0), (q1, k1, v1), (q2, k2, v2))
    for g, (_, dil) in enumerate(GROUPS):
        for part in range(3):
            col = part * ATTN_WIDTH + g * GROUP_WIDTH
            for c in range(dil):
                for j in range(GROUP_WIDTH // LANES):
                    rows = qkv_s[col // LANES + j, pl.ds(c, tm // dil, stride=dil), :]
                    outs[g][part][c, :, j * LANES:(j + 1) * LANES] = rows.astype(bf16)

    gas[...] = _silu(jnp.dot(h, w_ref[:, C_QKV:C_GA], preferred_element_type=f32))
    gm[...] = _sigmoid(jnp.dot(h, w_ref[:, C_GB:C_MA], preferred_element_type=f32))

    glu = jnp.dot(h, w_ref[:, C_GA:C_GLU], preferred_element_type=f32)
    u = glu[:, :CONV_WIDTH] * _sigmoid(glu[:, CONV_WIDTH:])

    @pl.when(i == 0)
    def _():
        ubuf[0:CARRY_ROWS, :] = jnp.zeros((CARRY_ROWS, CONV_WIDTH), f32)

    ubuf[CARRY_ROWS:CARRY_ROWS + tm, :] = u
    off = CARRY_ROWS - (CONV_KERNEL - 1)
    c = jnp.zeros((tm, CONV_WIDTH), f32) + cb_ref[...]
    for t in range(CONV_KERNEL):
        c = c + cw_ref[t:t + 1, :] * ubuf[off + t:off + t + tm, :]
    last = ubuf[tm:tm + CARRY_ROWS, :]
    ubuf[0:CARRY_ROWS, :] = last
    ulast[...] = last

    gb = jnp.dot(h, w_ref[:, C_GLU:C_GB], preferred_element_type=f32)
    mbp = jnp.dot(h, w_ref[:, C_MA:C_MB], preferred_element_type=f32)
    mb[...] = _conv_gate(c, gb, lg_ref[...], lb_ref[...], wob_ref[...], mbp)


def _const_spec(shape):
    nd = len(shape)
    return pl.BlockSpec(shape, lambda *_: (0,) * nd, pipeline_mode=pl.Buffered(1))


def _prompt_in(x, norm_pre, w_in, conv_w, conv_b, ln_g, ln_b, w_out_b, keep):
    n = x.shape[0]
    tm = ROW_TILE
    nt = n // tm
    keep_tiles = keep // tm
    row = lambda w: pl.BlockSpec((tm, w), lambda i: (i, 0))
    grp = lambda d: pl.BlockSpec((d, tm // d, GROUP_WIDTH), lambda i: (0, i, 0))
    last = pl.BlockSpec((tm, ATTN_WIDTH), lambda i: (jnp.maximum(i - (nt - keep_tiles), 0), 0))
    out_shape, out_specs = [], []
    for _, d in GROUPS:
        for _ in range(3):
            out_shape.append(jax.ShapeDtypeStruct((d, n // d, GROUP_WIDTH), bf16))
            out_specs.append(grp(d))
    out_shape += [jax.ShapeDtypeStruct((keep, ATTN_WIDTH), f32)] * 2
    out_specs += [last, last]
    out_shape += [jax.ShapeDtypeStruct((n, GROUP_WIDTH), f32),
                  jax.ShapeDtypeStruct((n, D_MODEL), f32),
                  jax.ShapeDtypeStruct((n, D_MODEL), f32),
                  jax.ShapeDtypeStruct((CARRY_ROWS, CONV_WIDTH), f32)]
    out_specs += [row(GROUP_WIDTH), row(D_MODEL), row(D_MODEL),
                  pl.BlockSpec((CARRY_ROWS, CONV_WIDTH), lambda i: (0, 0))]
    return pl.pallas_call(
        _prompt_in_body,
        grid=(nt,),
        in_specs=[row(D_MODEL), _const_spec(norm_pre.shape), _const_spec(w_in.shape),
                  _const_spec(conv_w.shape), _const_spec(conv_b.shape), _const_spec(ln_g.shape),
                  _const_spec(ln_b.shape), _const_spec(w_out_b.shape)],
        out_specs=out_specs,
        out_shape=out_shape,
        scratch_shapes=[pltpu.VMEM((C_QKV // LANES, tm, LANES), f32), pltpu.VMEM((tm + CARRY_ROWS, CONV_WIDTH), f32)],
        compiler_params=pltpu.CompilerParams(dimension_semantics=("arbitrary",),
                                             vmem_limit_bytes=VMEM_LIMIT),
        name="prompt_in",
    )(x, norm_pre, w_in, conv_w, conv_b, ln_g, ln_b, w_out_b)


def _prompt_attn_body(q_ref, kp_ref, kc_ref, vp_ref, vc_ref, bias_ref, o_ref, lse_ref):
    ib = pl.program_id(1)
    q = q_ref[0]
    k = jnp.concatenate([kp_ref[0], kc_ref[0]], axis=0)
    v = jnp.concatenate([vp_ref[0], vc_ref[0]], axis=0)
    lane = lax.broadcasted_iota(jnp.int32, (1, GROUP_WIDTH), 1)
    key = lax.broadcasted_iota(jnp.int32, (1, 2 * Q_BLOCK), 1)
    first = jnp.where((key < Q_BLOCK) & (ib == 0), NEG, 0.0).astype(f32)
    o = jnp.zeros((Q_BLOCK, GROUP_WIDTH), f32)
    lse = jnp.zeros((Q_BLOCK, GROUP_WIDTH), f32)
    for h in range(HEADS_PER_GROUP):
        head = (lane >= h * HEAD_DIM) & (lane < (h + 1) * HEAD_DIM)
        qh = jnp.where(head, q, jnp.zeros_like(q))
        s = lax.dot_general(qh, k, (((1,), (1,)), ((), ())), preferred_element_type=f32)
        s = s * SCALE + bias_ref[h] + first
        m = jnp.max(s, axis=-1, keepdims=True)
        p = jnp.exp(s - m)
        l = jnp.sum(p, axis=-1, keepdims=True)
        pv = jnp.dot(p.astype(bf16), v, preferred_element_type=f32)
        o = jnp.where(head, pv / l, o)
        lse = jnp.where(head, m + jnp.log(l), lse)
    o_ref[0] = o
    lse_ref[0] = lse


def _prompt_attn(q, k, v, bias, g):
    d, n, _ = q.shape
    cur = pl.BlockSpec((1, Q_BLOCK, GROUP_WIDTH), lambda c, i: (c, i, 0))
    prev = pl.BlockSpec((1, Q_BLOCK, GROUP_WIDTH), lambda c, i: (c, jnp.maximum(i - 1, 0), 0))
    out = jax.ShapeDtypeStruct((d, n, GROUP_WIDTH), f32)
    return pl.pallas_call(
        _prompt_attn_body,
        grid=(d, n // Q_BLOCK),
        in_specs=[cur, prev, cur, prev, cur, _const_spec(bias.shape)],
        out_specs=(cur, cur),
        out_shape=(out, out),
        compiler_params=pltpu.CompilerParams(dimension_semantics=("arbitrary", "arbitrary")),
        name=f"prompt_attn_{g}",
    )(q, k, k, v, v, bias)


def _prompt_out_body(o0, o1, o2, l0, l1, l2, gas, gm, mb, x, woa, wo, npost, y, so1, so2, sl1, sl2):
    tm = x.shape[0]
    for src, dst, d in ((o1, so1, GROUPS[1][1]), (l1, sl1, GROUPS[1][1]),
                        (o2, so2, GROUPS[2][1]), (l2, sl2, GROUPS[2][1])):
        for c in range(d):
            for j in range(GROUP_WIDTH // LANES):
                dst[j, pl.ds(c, tm // d, stride=d), :] = src[c, :, j * LANES:(j + 1) * LANES]
    rows = lambda s: jnp.concatenate([s[j] for j in range(GROUP_WIDTH // LANES)], axis=1)
    a = _combine((o0[0], rows(so1), rows(so2)), (l0[0], rows(sl1), rows(sl2)))
    y[...] = _tail(a, gas[...], gm[...], mb[...], x[...], woa[...], wo[...], npost[...])


def _prompt_out(os, ls, gas, gm, mb, x, w_out_a, w_o, norm_post):
    n = x.shape[0]
    tm = ROW_TILE
    row = lambda w: pl.BlockSpec((tm, w), lambda i: (i, 0))
    grp = lambda d: pl.BlockSpec((d, tm // d, GROUP_WIDTH), lambda i: (0, i, 0))
    gspecs = [grp(d) for _, d in GROUPS]
    return pl.pallas_call(
        _prompt_out_body,
        grid=(n // tm,),
        in_specs=gspecs + gspecs + [row(GROUP_WIDTH), row(D_MODEL), row(D_MODEL), row(D_MODEL),
                                    _const_spec(w_out_a.shape), _const_spec(w_o.shape),
                                    _const_spec(norm_post.shape)],
        out_specs=row(D_MODEL),
        out_shape=jax.ShapeDtypeStruct((n, D_MODEL), f32),
        scratch_shapes=[pltpu.VMEM((GROUP_WIDTH // LANES, tm, LANES), f32)] * 4,
        compiler_params=pltpu.CompilerParams(dimension_semantics=("arbitrary",),
                                             vmem_limit_bytes=VMEM_LIMIT),
        name="prompt_out",
    )(*os, *ls, gas, gm, mb, x, w_out_a, w_o, norm_post)


def _sample_in_body(x_ref, np_ref, w_ref, st_ref, cw_ref, cb_ref, lg_ref, lb_ref, wob_ref,
                    qt, kt, vt, gas, gm, mb, st_out):
    h = _rms(x_ref[...], np_ref[...]).astype(bf16)
    proj = jnp.dot(h, w_ref[...], preferred_element_type=f32)
    qt[...] = proj[:, 0:ATTN_WIDTH].T
    kt[...] = proj[:, ATTN_WIDTH:2 * ATTN_WIDTH].T
    vt[...] = proj[:, 2 * ATTN_WIDTH:C_QKV].T
    gas[...] = _silu(proj[:, C_QKV:C_GA])
    gm[...] = _sigmoid(proj[:, C_GB:C_MA])
    glu = proj[:, C_GA:C_GLU]
    u = glu[:, :CONV_WIDTH] * _sigmoid(glu[:, CONV_WIDTH:])
    hist = CONV_KERNEL - 1
    c = cb_ref[...] + cw_ref[hist:hist + 1, :] * u
    for t in range(hist):
        c = c + cw_ref[t:t + 1, :] * st_ref[t]
    for t in range(hist - 1):
        st_out[t] = st_ref[t + 1]
    st_out[hist - 1] = u
    mb[...] = _conv_gate(c, proj[:, C_GLU:C_GB], lg_ref[...], lb_ref[...], wob_ref[...],
                         proj[:, C_MA:C_MB])


def _sample_in(x, norm_pre, w_in, state, conv_w, conv_b, ln_g, ln_b, w_out_b):
    b = x.shape[0]
    t = jax.ShapeDtypeStruct((ATTN_WIDTH, b), f32)
    return pl.pallas_call(
        _sample_in_body,
        out_shape=(t, t, t,
                   jax.ShapeDtypeStruct((b, GROUP_WIDTH), f32),
                   jax.ShapeDtypeStruct((b, D_MODEL), f32),
                   jax.ShapeDtypeStruct((b, D_MODEL), f32),
                   jax.ShapeDtypeStruct(state.shape, f32)),
        compiler_params=pltpu.CompilerParams(vmem_limit_bytes=VMEM_LIMIT),
        name="sample_in",
    )(x, norm_pre, w_in, state, conv_w, conv_b, ln_g, ln_b, w_out_b)


def _sample_cache_body(kc_ref, vc_ref, qt_ref, kt_ref, vt_ref, bias_ref, bias0_ref,
                       ko_ref, vo_ref, o_ref, lse_ref):
    b = pl.program_id(1)
    w = kc_ref.shape[-1]
    nb = qt_ref.shape[-1]

    @pl.when(b == 0)
    def _():
        o_ref[...] = jnp.zeros(o_ref.shape, f32)
        lse_ref[...] = jnp.zeros(lse_ref.shape, f32)

    sel = lax.broadcasted_iota(jnp.int32, (HEAD_DIM, nb), 1) == b
    sel_row = lax.broadcasted_iota(jnp.int32, (1, nb), 1) == b
    newest = lax.broadcasted_iota(jnp.int32, (HEAD_DIM, w), 1) == w - 1
    pick = lambda ref, h: jnp.sum(jnp.where(sel, ref[h], 0.0), axis=1, keepdims=True)
    for h in range(HEADS_PER_GROUP):
        qc, kn, vn = pick(qt_ref, h), pick(kt_ref, h), pick(vt_ref, h)
        kc = kc_ref[0, h]
        vc = vc_ref[0, h]
        s = jnp.sum(kc * qc, axis=0, keepdims=True) * SCALE + bias_ref[h]
        s_new = jnp.sum(kn * qc, axis=0, keepdims=True) * SCALE + bias0_ref[h][:, 0:1]
        m = jnp.maximum(jnp.max(s, axis=1, keepdims=True), s_new)
        p = jnp.exp(s - m)
        p_new = jnp.exp(s_new - m)
        l = jnp.sum(p, axis=1, keepdims=True) + p_new
        o = (jnp.sum(vc * p, axis=1, keepdims=True) + vn * p_new) / l
        o_ref[h] = jnp.where(sel, o, o_ref[h])
        lse_ref[h] = jnp.where(sel_row, m + jnp.log(l), lse_ref[h])
        ko_ref[0, h] = jnp.where(newest, kn, pltpu.roll(kc, w - 1, axis=1))
        vo_ref[0, h] = jnp.where(newest, vn, pltpu.roll(vc, w - 1, axis=1))


def _sample_cache(kc, vc, qt, kt, vt, bias, bias0):
    nb, nh, hd, w = kc.shape
    cache = pl.BlockSpec((1, HEADS_PER_GROUP, hd, w), lambda g, b: (b, g, 0, 0))
    grp = lambda r, c: pl.BlockSpec((HEADS_PER_GROUP, r, c), lambda g, b: (g, 0, 0))
    return pl.pallas_call(
        _sample_cache_body,
        grid=(N_GROUPS, nb),
        in_specs=[cache, cache, grp(hd, nb), grp(hd, nb), grp(hd, nb), grp(1, w), grp(1, 128)],
        out_specs=(cache, cache, grp(hd, nb), grp(1, nb)),
        out_shape=(jax.ShapeDtypeStruct(kc.shape, f32), jax.ShapeDtypeStruct(vc.shape, f32),
                   jax.ShapeDtypeStruct((nh, hd, nb), f32), jax.ShapeDtypeStruct((nh, 1, nb), f32)),
        compiler_params=pltpu.CompilerParams(dimension_semantics=("arbitrary", "arbitrary"),
                                             vmem_limit_bytes=VMEM_LIMIT),
        name="sample_cache",
    )(kc, vc, qt, kt, vt, bias, bias0)


def _sample_out_body(ot, lt, gas, gm, mb, x, woa, wo, npost, y):
    nb = x.shape[0]
    os = [ot[g * HEADS_PER_GROUP:(g + 1) * HEADS_PER_GROUP] for g in range(N_GROUPS)]
    ls = [lt[g * HEADS_PER_GROUP:(g + 1) * HEADS_PER_GROUP] for g in range(N_GROUPS)]
    a = _combine(os, ls).reshape(GROUP_WIDTH, nb).T
    y[...] = _tail(a, gas[...], gm[...], mb[...], x[...], woa[...], wo[...], npost[...])


def _sample_out(ot, lt, gas, gm, mb, x, w_out_a, w_o, norm_post):
    return pl.pallas_call(
        _sample_out_body,
        out_shape=jax.ShapeDtypeStruct(x.shape, f32),
        name="sample_out",
    )(ot, lt, gas, gm, mb, x, w_out_a, w_o, norm_post)


def _t5_buckets(dist):
    n = np.asarray(dist, dtype=np.int64)
    large = MAX_EXACT + (np.log(np.maximum(n, 1) / MAX_EXACT) / np.log(REL_MAX_DIST / MAX_EXACT)
                         * (N_BUCKETS - MAX_EXACT)).astype(np.int64)
    large = np.minimum(large, N_BUCKETS - 1)
    return np.where(n < MAX_EXACT, n, large).astype(np.int32)


def _prompt_bias(rel_bias, g):
    _, dil = GROUPS[g]
    j = np.arange(Q_BLOCK)[:, None] + Q_BLOCK - np.arange(2 * Q_BLOCK)[None, :]
    valid = (j >= 0) & (j <= KEYS_PER_QUERY)
    buckets = _t5_buckets(dil * np.clip(j, 0, KEYS_PER_QUERY))
    tab = jnp.take(rel_bias[:, g * HEADS_PER_GROUP:(g + 1) * HEADS_PER_GROUP], jnp.asarray(buckets), axis=0)
    tab = jnp.where(jnp.asarray(valid)[:, :, None], tab, NEG)
    return jnp.transpose(tab, (2, 0, 1))


def _sample_bias(rel_bias, w_buf):
    dist = w_buf - np.arange(w_buf)
    rows = []
    for g, (window, dil) in enumerate(GROUPS):
        valid = (dist % dil == 0) & (dist <= window)
        tab = jnp.take(rel_bias[:, g * HEADS_PER_GROUP:(g + 1) * HEADS_PER_GROUP],
                       jnp.asarray(_t5_buckets(dist)), axis=0)
        rows.append(jnp.where(jnp.asarray(valid)[:, None], tab, NEG).T)
    bias = jnp.concatenate(rows, axis=0)[:, None, :]
    bias0 = jnp.broadcast_to(rel_bias[int(_t5_buckets(0)), :][:, None, None], (N_HEADS, 1, 128))
    return bias, bias0


def kernel(x_prompt, x_sample, cache_k, cache_v, state_conv, rel_bias, norm_pre, w_in, w_out_a, conv_w,
           conv_b, ln_g, ln_b, w_out_b, w_o, norm_post):
    assert x_prompt.shape[0] == 1 and cache_k.shape[0] == 1 and x_sample.shape[1] == 1
    n = x_prompt.shape[1]
    nb, w_buf = cache_k.shape[1], cache_k.shape[2]
    keep = min(GROUPS[-1][0], n)
    w_in_b, woa_b, wob_b, wo_b = (w[0].astype(bf16) for w in (w_in, w_out_a, w_out_b, w_o))
    cw, cb, lg, lb = conv_w[0], conv_b, ln_g, ln_b
    xp = x_prompt[0]

    (q0, k0, v0, q1, k1, v1, q2, k2, v2, klast, vlast, gas, gm, mb, ulast) = _prompt_in(
        xp, norm_pre, w_in_b, cw, cb, lg, lb, wob_b, keep)
    os, ls = [], []
    for g, (q, k, v) in enumerate(((q0, k0, v0), (q1, k1, v1), (q2, k2, v2))):
        o, l = _prompt_attn(q, k, v, _prompt_bias(rel_bias, g), g)
        os.append(o)
        ls.append(l)
    y_prompt = _prompt_out(os, ls, gas, gm, mb, xp, woa_b, wo_b, norm_post)

    xs = x_sample[:, 0, :]
    state = jnp.transpose(state_conv[0], (1, 0, 2))
    qt, kt, vt, gas_s, gm_s, mb_s, state_new = _sample_in(xs, norm_pre, w_in_b, state, cw, cb, lg, lb, wob_b)
    kc = jnp.transpose(cache_k[0], (0, 2, 3, 1))
    vc = jnp.transpose(cache_v[0], (0, 2, 3, 1))
    bias, bias0 = _sample_bias(rel_bias, w_buf)
    heads = lambda t: t.reshape(N_HEADS, HEAD_DIM, nb)
    ko, vo, ot, lt = _sample_cache(kc, vc, heads(qt), heads(kt), heads(vt), bias, bias0)
    y_sample = _sample_out(ot, lt, gas_s, gm_s, mb_s, xs, woa_b, wo_b, norm_post)

    back = lambda t: jnp.transpose(t, (0, 3, 1, 2))[None]
    return (y_prompt[None], y_sample[:, None, :],
            klast.reshape(1, 1, keep, N_HEADS, HEAD_DIM), vlast.reshape(1, 1, keep, N_HEADS, HEAD_DIM),
            ulast[CARRY_ROWS - (CONV_KERNEL - 1):][None, None],
            back(ko), back(vo), jnp.transpose(state_new, (1, 0, 2))[None])
```

```python
import functools

import numpy as np
import jax
import jax.numpy as jnp
from jax import lax
from jax.experimental import pallas as pl
from jax.experimental.pallas import tpu as pltpu

D_MODEL = 1024
GROUPS = ((128, 1), (512, 4), (2048, 16))
N_GROUPS = len(GROUPS)
HEADS_PER_GROUP = 4
N_HEADS = N_GROUPS * HEADS_PER_GROUP
HEAD_DIM = 64
GROUP_WIDTH = HEADS_PER_GROUP * HEAD_DIM
ATTN_WIDTH = N_HEADS * HEAD_DIM
KEYS_PER_QUERY = 128
N_BUCKETS = 32
MAX_EXACT = 16
REL_MAX_DIST = 2048
CONV_WIDTH = D_MODEL // 2
CONV_KERNEL = 31
EPS = 1e-6
NEG = -1e30
SCALE = HEAD_DIM ** -0.5

C_QKV = 3 * ATTN_WIDTH
C_GA = C_QKV + GROUP_WIDTH
C_GLU = C_GA + 2 * CONV_WIDTH
C_GB = C_GLU + CONV_WIDTH
C_MA = C_GB + D_MODEL
C_MB = C_MA + D_MODEL

ROW_TILE = 512
Q_BLOCK = 128
ATTN_TILE = 512
CACHE_BATCH = 2
LANES = 128
CARRY_ROWS = 32
VMEM_LIMIT = 52 * 1024 * 1024

f32 = jnp.float32
bf16 = jnp.bfloat16


def _rms(x, g):
    return x * lax.rsqrt(jnp.mean(x * x, axis=-1, keepdims=True) + EPS) * g


def _sigmoid(x):
    return 1.0 / (1.0 + jnp.exp(-x))


def _silu(x):
    return x * _sigmoid(x)


def _layernorm(c, g, b):
    mu = jnp.mean(c, axis=-1, keepdims=True)
    var = jnp.mean(jnp.square(c - mu), axis=-1, keepdims=True)
    return (c - mu) * lax.rsqrt(var + EPS) * g + b


def _mm(a, w):
    return jnp.dot(a.astype(bf16), w, preferred_element_type=f32)


def _conv_gate(c, gb, lg, lb, wob, mbp):
    c = _silu(_layernorm(c, lg, lb)) * _silu(gb)
    return _sigmoid(mbp) * _mm(c, wob)


def _tail(a, gas, gm, mb, x, woa, wo, npost):
    y_a = _mm(a * gas, woa)
    merged = gm * y_a + mb
    return x + _rms(_mm(merged, wo), npost)


def _combine(os, ls):
    m = jnp.maximum(jnp.maximum(ls[0], ls[1]), ls[2])
    es = [jnp.exp(l - m) for l in ls]
    den = es[0] + es[1] + es[2]
    return (es[0] * os[0] + es[1] * os[1] + es[2] * os[2]) / den


def _prompt_in_body(x_ref, np_ref, w_ref, cw_ref, cb_ref, lg_ref, lb_ref, wob_ref,
                    q0, k0, v0, q1, k1, v1, q2, k2, v2, klast, vlast, gas, gm, mb, ulast,
                    qkv_s, ubuf):
    i = pl.program_id(0)
    tm = x_ref.shape[0]
    h = _rms(x_ref[...], np_ref[...]).astype(bf16)

    qkv = jnp.dot(h, w_ref[:, 0:C_QKV], preferred_element_type=f32)
    klast[...] = qkv[:, ATTN_WIDTH:2 * ATTN_WIDTH]
    vlast[...] = qkv[:, 2 * ATTN_WIDTH:3 * ATTN_WIDTH]
    for j in range(C_QKV // LANES):
        qkv_s[j] = qkv[:, j * LANES:(j + 1) * LANES]
    outs = ((q0, k0, v0), (q1, k1, v1), (q2, k2, v2))
    for g, (_, dil) in enumerate(GROUPS):
        for part in range(3):
            col = part * ATTN_WIDTH + g * GROUP_WIDTH
            for c in range(dil):
                for j in range(GROUP_WIDTH // LANES):
                    rows = qkv_s[col // LANES + j, pl.ds(c, tm // dil, stride=dil), :]
                    outs[g][part][c, :, j * LANES:(j + 1) * LANES] = rows.astype(bf16)

    gas[...] = _silu(jnp.dot(h, w_ref[:, C_QKV:C_GA], preferred_element_type=f32))
    gm[...] = _sigmoid(jnp.dot(h, w_ref[:, C_GB:C_MA], preferred_element_type=f32))

    glu = jnp.dot(h, w_ref[:, C_GA:C_GLU], preferred_element_type=f32)
    u = glu[:, :CONV_WIDTH] * _sigmoid(glu[:, CONV_WIDTH:])

    @pl.when(i == 0)
    def _():
        ubuf[0:CARRY_ROWS, :] = jnp.zeros((CARRY_ROWS, CONV_WIDTH), f32)

    ubuf[CARRY_ROWS:CARRY_ROWS + tm, :] = u
    off = CARRY_ROWS - (CONV_KERNEL - 1)
    c = jnp.zeros((tm, CONV_WIDTH), f32) + cb_ref[...]
    for t in range(CONV_KERNEL):
        c = c + cw_ref[t:t + 1, :] * ubuf[off + t:off + t + tm, :]
    last = ubuf[tm:tm + CARRY_ROWS, :]
    ubuf[0:CARRY_ROWS, :] = last
    ulast[...] = last

    gb = jnp.dot(h, w_ref[:, C_GLU:C_GB], preferred_element_type=f32)
    mbp = jnp.dot(h, w_ref[:, C_MA:C_MB], preferred_element_type=f32)
    mb[...] = _conv_gate(c, gb, lg_ref[...], lb_ref[...], wob_ref[...], mbp)


def _const_spec(shape):
    nd = len(shape)
    return pl.BlockSpec(shape, lambda *_: (0,) * nd, pipeline_mode=pl.Buffered(1))


def _prompt_in(x, norm_pre, w_in, conv_w, conv_b, ln_g, ln_b, w_out_b, keep):
    n = x.shape[0]
    tm = ROW_TILE
    nt = n // tm
    keep_tiles = keep // tm
    row = lambda w: pl.BlockSpec((tm, w), lambda i: (i, 0))
    grp = lambda d: pl.BlockSpec((d, tm // d, GROUP_WIDTH), lambda i: (0, i, 0))
    last = pl.BlockSpec((tm, ATTN_WIDTH), lambda i: (jnp.maximum(i - (nt - keep_tiles), 0), 0))
    out_shape, out_specs = [], []
    for _, d in GROUPS:
        for _ in range(3):
            out_shape.append(jax.ShapeDtypeStruct((d, n // d, GROUP_WIDTH), bf16))
            out_specs.append(grp(d))
    out_shape += [jax.ShapeDtypeStruct((keep, ATTN_WIDTH), f32)] * 2
    out_specs += [last, last]
    out_shape += [jax.ShapeDtypeStruct((n, GROUP_WIDTH), f32),
                  jax.ShapeDtypeStruct((n, D_MODEL), f32),
                  jax.ShapeDtypeStruct((n, D_MODEL), f32),
                  jax.ShapeDtypeStruct((CARRY_ROWS, CONV_WIDTH), f32)]
    out_specs += [row(GROUP_WIDTH), row(D_MODEL), row(D_MODEL),
                  pl.BlockSpec((CARRY_ROWS, CONV_WIDTH), lambda i: (0, 0))]
    return pl.pallas_call(
        _prompt_in_body,
        grid=(nt,),
        in_specs=[row(D_MODEL), _const_spec(norm_pre.shape), _const_spec(w_in.shape),
                  _const_spec(conv_w.shape), _const_spec(conv_b.shape), _const_spec(ln_g.shape),
                  _const_spec(ln_b.shape), _const_spec(w_out_b.shape)],
        out_specs=out_specs,
        out_shape=out_shape,
        scratch_shapes=[pltpu.VMEM((C_QKV // LANES, tm, LANES), f32), pltpu.VMEM((tm + CARRY_ROWS, CONV_WIDTH), f32)],
        compiler_params=pltpu.CompilerParams(dimension_semantics=("arbitrary",),
                                             vmem_limit_bytes=VMEM_LIMIT),
        name="prompt_in",
    )(x, norm_pre, w_in, conv_w, conv_b, ln_g, ln_b, w_out_b)


def _prompt_attn_body(q_ref, kp_ref, kc_ref, vp_ref, vc_ref, bias_ref, o_ref, lse_ref, *, tiles_per_seq):
    first_tile = pl.program_id(0) % tiles_per_seq == 0
    k_all = jnp.concatenate([kp_ref[...], kc_ref[...]], axis=0)
    v_all = jnp.concatenate([vp_ref[...], vc_ref[...]], axis=0)
    lane = lax.broadcasted_iota(jnp.int32, (1, GROUP_WIDTH), 1)
    key = lax.broadcasted_iota(jnp.int32, (1, 2 * Q_BLOCK), 1)
    first = jnp.where((key < Q_BLOCK) & first_tile, NEG, 0.0).astype(f32)
    heads = [(lane >= h * HEAD_DIM) & (lane < (h + 1) * HEAD_DIM) for h in range(HEADS_PER_GROUP)]

    def scores(j):
        q = q_ref[j * Q_BLOCK:(j + 1) * Q_BLOCK, :]
        k = k_all[j * Q_BLOCK:(j + 2) * Q_BLOCK]
        out = []
        for h in range(HEADS_PER_GROUP):
            qh = jnp.where(heads[h], q, jnp.zeros_like(q))
            s = lax.dot_general(qh, k, (((1,), (1,)), ((), ())), preferred_element_type=f32)
            s = s * SCALE + bias_ref[h]
            out.append(s + first if j == 0 else s)
        return out

    def finish(j, ss):
        v = v_all[j * Q_BLOCK:(j + 2) * Q_BLOCK]
        ms = [jnp.max(s, axis=-1, keepdims=True) for s in ss]
        ps = [jnp.exp(s - m) for s, m in zip(ss, ms)]
        ls = [jnp.sum(p, axis=-1, keepdims=True) for p in ps]
        o = jnp.zeros((Q_BLOCK, GROUP_WIDTH), f32)
        lse = jnp.zeros((Q_BLOCK, GROUP_WIDTH), f32)
        for h in range(HEADS_PER_GROUP):
            pv = jnp.dot(ps[h].astype(bf16), v, preferred_element_type=f32)
            o = jnp.where(heads[h], pv / ls[h], o)
            lse = jnp.where(heads[h], ms[h] + jnp.log(ls[h]), lse)
        o_ref[j * Q_BLOCK:(j + 1) * Q_BLOCK, :] = o
        lse_ref[j * Q_BLOCK:(j + 1) * Q_BLOCK, :] = lse

    n_blocks = ATTN_TILE // Q_BLOCK
    ss = scores(0)
    for j in range(n_blocks):
        ss_next = scores(j + 1) if j + 1 < n_blocks else None
        finish(j, ss)
        ss = ss_next


def _prompt_attn(q, k, v, bias, g):
    d, n, _ = q.shape
    q, k, v = (t.reshape(d * n, GROUP_WIDTH) for t in (q, k, v))
    per_tile = ATTN_TILE // Q_BLOCK
    cur = pl.BlockSpec((ATTN_TILE, GROUP_WIDTH), lambda t: (t, 0))
    prev = pl.BlockSpec((Q_BLOCK, GROUP_WIDTH), lambda t: (jnp.maximum(t * per_tile - 1, 0), 0))
    out = jax.ShapeDtypeStruct((d * n, GROUP_WIDTH), f32)
    o, lse = pl.pallas_call(
        functools.partial(_prompt_attn_body, tiles_per_seq=n // ATTN_TILE),
        grid=(d * n // ATTN_TILE,),
        in_specs=[cur, prev, cur, prev, cur, _const_spec(bias.shape)],
        out_specs=(cur, cur),
        out_shape=(out, out),
        compiler_params=pltpu.CompilerParams(dimension_semantics=("arbitrary",)),
        name=f"prompt_attn_{g}",
    )(q, k, k, v, v, bias)
    return o.reshape(d, n, GROUP_WIDTH), lse.reshape(d, n, GROUP_WIDTH)


def _prompt_out_body(o0, o1, o2, l0, l1, l2, gas, gm, mb, x, woa, wo, npost, y, so1, so2, sl1, sl2):
    tm = x.shape[0]
    for src, dst, d in ((o1, so1, GROUPS[1][1]), (l1, sl1, GROUPS[1][1]),
                        (o2, so2, GROUPS[2][1]), (l2, sl2, GROUPS[2][1])):
        for c in range(d):
            for j in range(GROUP_WIDTH // LANES):
                dst[j, pl.ds(c, tm // d, stride=d), :] = src[c, :, j * LANES:(j + 1) * LANES]
    rows = lambda s: jnp.concatenate([s[j] for j in range(GROUP_WIDTH // LANES)], axis=1)
    a = _combine((o0[0], rows(so1), rows(so2)), (l0[0], rows(sl1), rows(sl2)))
    y[...] = _tail(a, gas[...], gm[...], mb[...], x[...], woa[...], wo[...], npost[...])


def _prompt_out(os, ls, gas, gm, mb, x, w_out_a, w_o, norm_post):
    n = x.shape[0]
    tm = ROW_TILE
    row = lambda w: pl.BlockSpec((tm, w), lambda i: (i, 0))
    grp = lambda d: pl.BlockSpec((d, tm // d, GROUP_WIDTH), lambda i: (0, i, 0))
    gspecs = [grp(d) for _, d in GROUPS]
    return pl.pallas_call(
        _prompt_out_body,
        grid=(n // tm,),
        in_specs=gspecs + gspecs + [row(GROUP_WIDTH), row(D_MODEL), row(D_MODEL), row(D_MODEL),
                                    _const_spec(w_out_a.shape), _const_spec(w_o.shape),
                                    _const_spec(norm_post.shape)],
        out_specs=row(D_MODEL),
        out_shape=jax.ShapeDtypeStruct((n, D_MODEL), f32),
        scratch_shapes=[pltpu.VMEM((GROUP_WIDTH // LANES, tm, LANES), f32)] * 4,
        compiler_params=pltpu.CompilerParams(dimension_semantics=("arbitrary",),
                                             vmem_limit_bytes=VMEM_LIMIT),
        name="prompt_out",
    )(*os, *ls, gas, gm, mb, x, w_out_a, w_o, norm_post)


def _sample_in_body(x_ref, np_ref, w_ref, st_ref, cw_ref, cb_ref, lg_ref, lb_ref, wob_ref,
                    qt, kt, vt, gas, gm, mb, st_out):
    h = _rms(x_ref[...], np_ref[...]).astype(bf16)
    proj = jnp.dot(h, w_ref[...], preferred_element_type=f32)
    qt[...] = proj[:, 0:ATTN_WIDTH].T
    kt[...] = proj[:, ATTN_WIDTH:2 * ATTN_WIDTH].T
    vt[...] = proj[:, 2 * ATTN_WIDTH:C_QKV].T
    gas[...] = _silu(proj[:, C_QKV:C_GA])
    gm[...] = _sigmoid(proj[:, C_GB:C_MA])
    glu = proj[:, C_GA:C_GLU]
    u = glu[:, :CONV_WIDTH] * _sigmoid(glu[:, CONV_WIDTH:])
    hist = CONV_KERNEL - 1
    c = cb_ref[...] + cw_ref[hist:hist + 1, :] * u
    for t in range(hist):
        c = c + cw_ref[t:t + 1, :] * st_ref[t]
    for t in range(hist - 1):
        st_out[t] = st_ref[t + 1]
    st_out[hist - 1] = u
    mb[...] = _conv_gate(c, proj[:, C_GLU:C_GB], lg_ref[...], lb_ref[...], wob_ref[...],
                         proj[:, C_MA:C_MB])


def _sample_in(x, norm_pre, w_in, state, conv_w, conv_b, ln_g, ln_b, w_out_b):
    b = x.shape[0]
    t = jax.ShapeDtypeStruct((ATTN_WIDTH, b), f32)
    return pl.pallas_call(
        _sample_in_body,
        out_shape=(t, t, t,
                   jax.ShapeDtypeStruct((b, GROUP_WIDTH), f32),
                   jax.ShapeDtypeStruct((b, D_MODEL), f32),
                   jax.ShapeDtypeStruct((b, D_MODEL), f32),
                   jax.ShapeDtypeStruct(state.shape, f32)),
        compiler_params=pltpu.CompilerParams(vmem_limit_bytes=VMEM_LIMIT),
        name="sample_in",
    )(x, norm_pre, w_in, state, conv_w, conv_b, ln_g, ln_b, w_out_b)


def _sample_cache_body(kc_ref, vc_ref, qt_ref, kt_ref, vt_ref, bias_ref, bias0_ref,
                       ko_ref, vo_ref, o_ref, lse_ref):
    step = pl.program_id(1)
    w = kc_ref.shape[-1]
    nb = qt_ref.shape[-1]

    @pl.when(step == 0)
    def _():
        o_ref[...] = jnp.zeros(o_ref.shape, f32)
        lse_ref[...] = jnp.zeros(lse_ref.shape, f32)

    seq = lax.broadcasted_iota(jnp.int32, (1, 1, nb), 2)
    newest = lax.broadcasted_iota(jnp.int32, (1, 1, w), 2) == w - 1
    bias0 = bias0_ref[...][:, :, 0:1]
    for i in range(kc_ref.shape[0]):
        sel = seq == step * kc_ref.shape[0] + i
        pick = lambda ref: jnp.sum(jnp.where(sel, ref[...], 0.0), axis=2, keepdims=True)
        qc, kn, vn = pick(qt_ref), pick(kt_ref), pick(vt_ref)
        kc = kc_ref[i]
        vc = vc_ref[i]
        s = jnp.sum(kc * qc, axis=1, keepdims=True) * SCALE + bias_ref[...]
        s_new = jnp.sum(kn * qc, axis=1, keepdims=True) * SCALE + bias0
        m = jnp.maximum(jnp.max(s, axis=2, keepdims=True), s_new)
        p = jnp.exp(s - m)
        p_new = jnp.exp(s_new - m)
        l = jnp.sum(p, axis=2, keepdims=True) + p_new
        o = (jnp.sum(vc * p, axis=2, keepdims=True) + vn * p_new) / l
        o_ref[...] = jnp.where(sel, o, o_ref[...])
        lse_ref[...] = jnp.where(sel, m + jnp.log(l), lse_ref[...])
        ko_ref[i] = jnp.where(newest, kn, pltpu.roll(kc, w - 1, axis=2))
        vo_ref[i] = jnp.where(newest, vn, pltpu.roll(vc, w - 1, axis=2))


def _sample_cache(kc, vc, qt, kt, vt, bias, bias0):
    nb, nh, hd, w = kc.shape
    cache = pl.BlockSpec((CACHE_BATCH, HEADS_PER_GROUP, hd, w), lambda g, b: (b, g, 0, 0))
    grp = lambda r, c: pl.BlockSpec((HEADS_PER_GROUP, r, c), lambda g, b: (g, 0, 0))
    return pl.pallas_call(
        _sample_cache_body,
        grid=(N_GROUPS, nb // CACHE_BATCH),
        in_specs=[cache, cache, grp(hd, nb), grp(hd, nb), grp(hd, nb), grp(1, w), grp(1, 128)],
        out_specs=(cache, cache, grp(hd, nb), grp(1, nb)),
        out_shape=(jax.ShapeDtypeStruct(kc.shape, f32), jax.ShapeDtypeStruct(vc.shape, f32),
                   jax.ShapeDtypeStruct((nh, hd, nb), f32), jax.ShapeDtypeStruct((nh, 1, nb), f32)),
        compiler_params=pltpu.CompilerParams(dimension_semantics=("arbitrary", "arbitrary"),
                                             vmem_limit_bytes=VMEM_LIMIT),
        name="sample_cache",
    )(kc, vc, qt, kt, vt, bias, bias0)


def _sample_out_body(ot, lt, gas, gm, mb, x, woa, wo, npost, y):
    nb = x.shape[0]
    os = [ot[g * HEADS_PER_GROUP:(g + 1) * HEADS_PER_GROUP] for g in range(N_GROUPS)]
    ls = [lt[g * HEADS_PER_GROUP:(g + 1) * HEADS_PER_GROUP] for g in range(N_GROUPS)]
    a = _combine(os, ls).reshape(GROUP_WIDTH, nb).T
    y[...] = _tail(a, gas[...], gm[...], mb[...], x[...], woa[...], wo[...], npost[...])


def _sample_out(ot, lt, gas, gm, mb, x, w_out_a, w_o, norm_post):
    return pl.pallas_call(
        _sample_out_body,
        out_shape=jax.ShapeDtypeStruct(x.shape, f32),
        name="sample_out",
    )(ot, lt, gas, gm, mb, x, w_out_a, w_o, norm_post)


def _t5_buckets(dist):
    n = np.asarray(dist, dtype=np.int64)
    large = MAX_EXACT + (np.log(np.maximum(n, 1) / MAX_EXACT) / np.log(REL_MAX_DIST / MAX_EXACT)
                         * (N_BUCKETS - MAX_EXACT)).astype(np.int64)
    large = np.minimum(large, N_BUCKETS - 1)
    return np.where(n < MAX_EXACT, n, large).astype(np.int32)


def _bias_rows(rel_bias, g, buckets, valid):
    onehot = (jnp.asarray(buckets)[..., None] == jnp.arange(N_BUCKETS, dtype=jnp.int32)).astype(f32)
    cols = rel_bias[:, g * HEADS_PER_GROUP:(g + 1) * HEADS_PER_GROUP]
    tab = jnp.einsum('...b,bh->h...', onehot, cols, precision=lax.Precision.HIGHEST)
    return jnp.where(jnp.asarray(valid)[None], tab, NEG)


def _prompt_bias(rel_bias, g):
    _, dil = GROUPS[g]
    j = np.arange(Q_BLOCK)[:, None] + Q_BLOCK - np.arange(2 * Q_BLOCK)[None, :]
    valid = (j >= 0) & (j <= KEYS_PER_QUERY)
    return _bias_rows(rel_bias, g, _t5_buckets(dil * np.clip(j, 0, KEYS_PER_QUERY)), valid)


def _sample_bias(rel_bias, w_buf):
    dist = w_buf - np.arange(w_buf)
    rows = []
    for g, (window, dil) in enumerate(GROUPS):
        valid = (dist % dil == 0) & (dist <= window)
        rows.append(_bias_rows(rel_bias, g, _t5_buckets(dist), valid))
    bias = jnp.concatenate(rows, axis=0)[:, None, :]
    bias0 = jnp.broadcast_to(rel_bias[int(_t5_buckets(0)), :][:, None, None], (N_HEADS, 1, 128))
    return bias, bias0


def kernel(x_prompt, x_sample, cache_k, cache_v, state_conv, rel_bias, norm_pre, w_in, w_out_a, conv_w,
           conv_b, ln_g, ln_b, w_out_b, w_o, norm_post):
    assert x_prompt.shape[0] == 1 and cache_k.shape[0] == 1 and x_sample.shape[1] == 1
    n = x_prompt.shape[1]
    nb, w_buf = cache_k.shape[1], cache_k.shape[2]
    keep = min(GROUPS[-1][0], n)
    w_in_b, woa_b, wob_b, wo_b = (w[0].astype(bf16) for w in (w_in, w_out_a, w_out_b, w_o))
    cw, cb, lg, lb = conv_w[0], conv_b, ln_g, ln_b
    xp = x_prompt[0]

    (q0, k0, v0, q1, k1, v1, q2, k2, v2, klast, vlast, gas, gm, mb, ulast) = _prompt_in(
        xp, norm_pre, w_in_b, cw, cb, lg, lb, wob_b, keep)
    os, ls = [], []
    for g, (q, k, v) in enumerate(((q0, k0, v0), (q1, k1, v1), (q2, k2, v2))):
        o, l = _prompt_attn(q, k, v, _prompt_bias(rel_bias, g), g)
        os.append(o)
        ls.append(l)
    y_prompt = _prompt_out(os, ls, gas, gm, mb, xp, woa_b, wo_b, norm_post)

    xs = x_sample[:, 0, :]
    state = jnp.transpose(state_conv[0], (1, 0, 2))
    qt, kt, vt, gas_s, gm_s, mb_s, state_new = _sample_in(xs, norm_pre, w_in_b, state, cw, cb, lg, lb, wob_b)
    kc = jnp.transpose(cache_k[0], (0, 2, 3, 1))
    vc = jnp.transpose(cache_v[0], (0, 2, 3, 1))
    bias, bias0 = _sample_bias(rel_bias, w_buf)
    heads = lambda t: t.reshape(N_HEADS, HEAD_DIM, nb)
    ko, vo, ot, lt = _sample_cache(kc, vc, heads(qt), heads(kt), heads(vt), bias, bias0)
    y_sample = _sample_out(ot, lt, gas_s, gm_s, mb_s, xs, woa_b, wo_b, norm_post)

    back = lambda t: jnp.transpose(t, (0, 3, 1, 2))[None]
    return (y_prompt[None], y_sample[:, None, :],
            klast.reshape(1, 1, keep, N_HEADS, HEAD_DIM), vlast.reshape(1, 1, keep, N_HEADS, HEAD_DIM),
            ulast[CARRY_ROWS - (CONV_KERNEL - 1):][None, None],
            back(ko), back(vo), jnp.transpose(state_new, (1, 0, 2))[None])
```

```python
import functools

import numpy as np
import jax
import jax.numpy as jnp
from jax import lax
from jax.experimental import pallas as pl
from jax.experimental.pallas import tpu as pltpu

D_MODEL = 1024
GROUPS = ((128, 1), (512, 4), (2048, 16))
N_GROUPS = len(GROUPS)
HEADS_PER_GROUP = 4
N_HEADS = N_GROUPS * HEADS_PER_GROUP
HEAD_DIM = 64
GROUP_WIDTH = HEADS_PER_GROUP * HEAD_DIM
ATTN_WIDTH = N_HEADS * HEAD_DIM
KEYS_PER_QUERY = 128
N_BUCKETS = 32
MAX_EXACT = 16
REL_MAX_DIST = 2048
CONV_WIDTH = D_MODEL // 2
CONV_KERNEL = 31
EPS = 1e-6
NEG = -1e30
SCALE = HEAD_DIM ** -0.5

C_QKV = 3 * ATTN_WIDTH
C_GA = C_QKV + GROUP_WIDTH
C_GLU = C_GA + 2 * CONV_WIDTH
C_GB = C_GLU + CONV_WIDTH
C_MA = C_GB + D_MODEL
C_MB = C_MA + D_MODEL

IN_TILE = 256
ROW_TILE = 512
Q_BLOCK = 128
ATTN_TILE = 512
LANES = 128
CARRY_ROWS = 32
VMEM_LIMIT = 52 * 1024 * 1024

f32 = jnp.float32
bf16 = jnp.bfloat16


def _rms(x, g):
    return x * lax.rsqrt(jnp.mean(x * x, axis=-1, keepdims=True) + EPS) * g


def _sigmoid(x):
    return 1.0 / (1.0 + jnp.exp(-x))


def _silu(x):
    return x * _sigmoid(x)


def _layernorm(c, g, b):
    mu = jnp.mean(c, axis=-1, keepdims=True)
    var = jnp.mean(jnp.square(c - mu), axis=-1, keepdims=True)
    return (c - mu) * lax.rsqrt(var + EPS) * g + b


def _mm(a, w):
    return jnp.dot(a.astype(bf16), w, preferred_element_type=f32)


def _conv_gate(c, gb, lg, lb, wob, mbp):
    c = _silu(_layernorm(c, lg, lb)) * _silu(gb)
    return _sigmoid(mbp) * _mm(c, wob)


def _tail(a, gas, gm, mb, x, woa, wo, npost):
    y_a = _mm(a * gas, woa)
    merged = gm * y_a + mb
    return x + _rms(_mm(merged, wo), npost)


def _combine(os, ls):
    m = jnp.maximum(jnp.maximum(ls[0], ls[1]), ls[2])
    es = [jnp.exp(l - m) for l in ls]
    den = es[0] + es[1] + es[2]
    return (es[0] * os[0] + es[1] * os[1] + es[2] * os[2]) / den


def _cache_attend_shift(kc, vc, g, seq, qt_ref, kt_ref, vt_ref, bias_ref, bias0_ref, o_ref, lse_ref):
    w = kc.shape[-1]
    hs = slice(g * HEADS_PER_GROUP, (g + 1) * HEADS_PER_GROUP)
    sel = lax.broadcasted_iota(jnp.int32, (1, 1, qt_ref.shape[-1]), 2) == seq
    newest = lax.broadcasted_iota(jnp.int32, (1, 1, w), 2) == w - 1
    pick = lambda ref: jnp.sum(jnp.where(sel, ref[hs], 0.0), axis=2, keepdims=True)
    qc, kn, vn = pick(qt_ref), pick(kt_ref), pick(vt_ref)
    s = jnp.sum(kc * qc, axis=1, keepdims=True) * SCALE + bias_ref[hs]
    s_new = jnp.sum(kn * qc, axis=1, keepdims=True) * SCALE + bias0_ref[hs][:, :, 0:1]
    m = jnp.maximum(jnp.max(s, axis=2, keepdims=True), s_new)
    p = jnp.exp(s - m)
    p_new = jnp.exp(s_new - m)
    l = jnp.sum(p, axis=2, keepdims=True) + p_new
    o = (jnp.sum(vc * p, axis=2, keepdims=True) + vn * p_new) / l
    o_ref[hs] = jnp.where(sel, o, o_ref[hs])
    lse_ref[hs] = jnp.where(sel, m + jnp.log(l), lse_ref[hs])
    return (jnp.where(newest, kn, pltpu.roll(kc, w - 1, axis=2)),
            jnp.where(newest, vn, pltpu.roll(vc, w - 1, axis=2)))


def _prompt_in_body(x_ref, np_ref, w_ref, cw_ref, cb_ref, lg_ref, lb_ref, wob_ref,
                    qt_ref, kt_ref, vt_ref, bias_ref, bias0_ref, kc_hbm, vc_hbm,
                    q0, k0, v0, q1, k1, v1, q2, k2, v2, klast, vlast, gas, gm, mb, ulast,
                    ot_ref, lt_ref, ko_hbm, vo_hbm,
                    qkv_s, ubuf, kbuf, vbuf, kobuf, vobuf, sem_in, sem_out, *, per_tile):
    i = pl.program_id(0)
    nt = pl.num_programs(0)
    tm = x_ref.shape[0]

    def block(t, j):
        return t * (per_tile // N_GROUPS) + j // N_GROUPS, j % N_GROUPS, j % 2

    def in_copies(t, j):
        seq, g, slot = block(t, j)
        hs = pl.ds(g * HEADS_PER_GROUP, HEADS_PER_GROUP)
        return (pltpu.make_async_copy(kc_hbm.at[seq, hs], kbuf.at[slot], sem_in.at[slot, 0]),
                pltpu.make_async_copy(vc_hbm.at[seq, hs], vbuf.at[slot], sem_in.at[slot, 1]))

    def out_copies(t, j):
        seq, g, slot = block(t, j)
        hs = pl.ds(g * HEADS_PER_GROUP, HEADS_PER_GROUP)
        return (pltpu.make_async_copy(kobuf.at[slot], ko_hbm.at[seq, hs], sem_out.at[slot, 0]),
                pltpu.make_async_copy(vobuf.at[slot], vo_hbm.at[seq, hs], sem_out.at[slot, 1]))

    def start(copies):
        for cp in copies:
            cp.start()

    def wait(copies):
        for cp in copies:
            cp.wait()

    @pl.when(i == 0)
    def _():
        ot_ref[...] = jnp.zeros(ot_ref.shape, f32)
        lt_ref[...] = jnp.zeros(lt_ref.shape, f32)
        ubuf[0:CARRY_ROWS, :] = jnp.zeros((CARRY_ROWS, CONV_WIDTH), f32)
        start(in_copies(i, 0))

    def cache_block(j):
        seq, g, slot = block(i, j)
        wait(in_copies(i, j))
        if j + 1 < per_tile:
            start(in_copies(i, j + 1))
        else:
            @pl.when(i + 1 < nt)
            def _():
                start(in_copies(i + 1, 0))
        if j >= 2:
            wait(out_copies(i, j - 2))
        else:
            @pl.when(i > 0)
            def _():
                wait(out_copies(i - 1, per_tile - 2 + j))
        ko, vo = _cache_attend_shift(kbuf[slot], vbuf[slot], g, seq, qt_ref, kt_ref, vt_ref,
                                     bias_ref, bias0_ref, ot_ref, lt_ref)
        kobuf[slot] = ko
        vobuf[slot] = vo
        start(out_copies(i, j))

    h = _rms(x_ref[...], np_ref[...]).astype(bf16)

    def project_qkv():
        qkv = jnp.dot(h, w_ref[:, 0:C_QKV], preferred_element_type=f32)
        klast[...] = qkv[:, ATTN_WIDTH:2 * ATTN_WIDTH]
        vlast[...] = qkv[:, 2 * ATTN_WIDTH:3 * ATTN_WIDTH]
        for j in range(C_QKV // LANES):
            qkv_s[j] = qkv[:, j * LANES:(j + 1) * LANES]

    def regroup_qkv():
        outs = ((q0, k0, v0), (q1, k1, v1), (q2, k2, v2))
        for g, (_, dil) in enumerate(GROUPS):
            for part in range(3):
                col = part * ATTN_WIDTH + g * GROUP_WIDTH
                for c in range(dil):
                    for j in range(GROUP_WIDTH // LANES):
                        rows = qkv_s[col // LANES + j, pl.ds(c, tm // dil, stride=dil), :]
                        outs[g][part][c, :, j * LANES:(j + 1) * LANES] = rows.astype(bf16)

    def gates():
        gas[...] = _silu(jnp.dot(h, w_ref[:, C_QKV:C_GA], preferred_element_type=f32))
        gm[...] = _sigmoid(jnp.dot(h, w_ref[:, C_GB:C_MA], preferred_element_type=f32))

    def glu_unit():
        glu = jnp.dot(h, w_ref[:, C_GA:C_GLU], preferred_element_type=f32)
        ubuf[CARRY_ROWS:CARRY_ROWS + tm, :] = glu[:, :CONV_WIDTH] * _sigmoid(glu[:, CONV_WIDTH:])

    def conv_branch():
        off = CARRY_ROWS - (CONV_KERNEL - 1)
        c = jnp.zeros((tm, CONV_WIDTH), f32) + cb_ref[...]
        for t in range(CONV_KERNEL):
            c = c + cw_ref[t:t + 1, :] * ubuf[off + t:off + t + tm, :]
        last = ubuf[tm:tm + CARRY_ROWS, :]
        ubuf[0:CARRY_ROWS, :] = last
        ulast[...] = last
        gb = jnp.dot(h, w_ref[:, C_GLU:C_GB], preferred_element_type=f32)
        mbp = jnp.dot(h, w_ref[:, C_MA:C_MB], preferred_element_type=f32)
        mb[...] = _conv_gate(c, gb, lg_ref[...], lb_ref[...], wob_ref[...], mbp)

    phases = (project_qkv, regroup_qkv, gates, glu_unit, conv_branch)
    done = 0
    for n, phase in enumerate(phases):
        upto = per_tile * (n + 1) // len(phases)
        for j in range(done, upto):
            cache_block(j)
        done = upto
        phase()

    @pl.when(i == nt - 1)
    def _():
        wait(out_copies(i, per_tile - 2))
        wait(out_copies(i, per_tile - 1))


def _const_spec(shape):
    nd = len(shape)
    return pl.BlockSpec(shape, lambda *_: (0,) * nd, pipeline_mode=pl.Buffered(1))


def _prompt_in(x, norm_pre, w_in, conv_w, conv_b, ln_g, ln_b, w_out_b, keep, qt, kt, vt, bias, bias0, kc, vc):
    n = x.shape[0]
    tm = IN_TILE
    nt = n // tm
    keep_tiles = keep // tm
    nb, nh, hd, w = kc.shape
    per_tile = nb * N_GROUPS // nt
    assert per_tile * nt == nb * N_GROUPS and per_tile % N_GROUPS == 0 and per_tile % 2 == 0
    row = lambda w: pl.BlockSpec((tm, w), lambda i: (i, 0))
    grp = lambda d: pl.BlockSpec((d, tm // d, GROUP_WIDTH), lambda i: (0, i, 0))
    last = pl.BlockSpec((tm, ATTN_WIDTH), lambda i: (jnp.maximum(i - (nt - keep_tiles), 0), 0))
    hbm = pl.BlockSpec(memory_space=pl.ANY)
    out_shape, out_specs = [], []
    for _, d in GROUPS:
        for _ in range(3):
            out_shape.append(jax.ShapeDtypeStruct((d, n // d, GROUP_WIDTH), bf16))
            out_specs.append(grp(d))
    out_shape += [jax.ShapeDtypeStruct((keep, ATTN_WIDTH), f32)] * 2
    out_specs += [last, last]
    out_shape += [jax.ShapeDtypeStruct((n, GROUP_WIDTH), f32),
                  jax.ShapeDtypeStruct((n, D_MODEL), f32),
                  jax.ShapeDtypeStruct((n, D_MODEL), f32),
                  jax.ShapeDtypeStruct((CARRY_ROWS, CONV_WIDTH), f32)]
    out_specs += [row(GROUP_WIDTH), row(D_MODEL), row(D_MODEL),
                  pl.BlockSpec((CARRY_ROWS, CONV_WIDTH), lambda i: (0, 0))]
    out_shape += [jax.ShapeDtypeStruct((nh, hd, nb), f32), jax.ShapeDtypeStruct((nh, 1, nb), f32),
                  jax.ShapeDtypeStruct(kc.shape, f32), jax.ShapeDtypeStruct(vc.shape, f32)]
    out_specs += [pl.BlockSpec((nh, hd, nb), lambda i: (0, 0, 0)), pl.BlockSpec((nh, 1, nb), lambda i: (0, 0, 0)),
                  hbm, hbm]
    cache_buf = pltpu.VMEM((2, HEADS_PER_GROUP, hd, w), f32)
    return pl.pallas_call(
        functools.partial(_prompt_in_body, per_tile=per_tile),
        grid=(nt,),
        in_specs=[row(D_MODEL), _const_spec(norm_pre.shape), _const_spec(w_in.shape),
                  _const_spec(conv_w.shape), _const_spec(conv_b.shape), _const_spec(ln_g.shape),
                  _const_spec(ln_b.shape), _const_spec(w_out_b.shape),
                  _const_spec(qt.shape), _const_spec(kt.shape), _const_spec(vt.shape),
                  _const_spec(bias.shape), _const_spec(bias0.shape), hbm, hbm],
        out_specs=out_specs,
        out_shape=out_shape,
        scratch_shapes=[pltpu.VMEM((C_QKV // LANES, tm, LANES), f32),
                        pltpu.VMEM((tm + CARRY_ROWS, CONV_WIDTH), f32),
                        cache_buf, cache_buf, cache_buf, cache_buf,
                        pltpu.SemaphoreType.DMA((2, 2)), pltpu.SemaphoreType.DMA((2, 2))],
        compiler_params=pltpu.CompilerParams(dimension_semantics=("arbitrary",),
                                             vmem_limit_bytes=VMEM_LIMIT),
        name="prompt_in",
    )(x, norm_pre, w_in, conv_w, conv_b, ln_g, ln_b, w_out_b, qt, kt, vt, bias, bias0, kc, vc)


def _prompt_attn_body(q_ref, kp_ref, kc_ref, vp_ref, vc_ref, bias_ref, o_ref, lse_ref, *, tiles_per_seq):
    first_tile = pl.program_id(0) % tiles_per_seq == 0
    k_all = jnp.concatenate([kp_ref[...], kc_ref[...]], axis=0)
    v_all = jnp.concatenate([vp_ref[...], vc_ref[...]], axis=0)
    lane = lax.broadcasted_iota(jnp.int32, (1, GROUP_WIDTH), 1)
    key = lax.broadcasted_iota(jnp.int32, (1, 2 * Q_BLOCK), 1)
    first = jnp.where((key < Q_BLOCK) & first_tile, NEG, 0.0).astype(f32)
    heads = [(lane >= h * HEAD_DIM) & (lane < (h + 1) * HEAD_DIM) for h in range(HEADS_PER_GROUP)]

    def scores(j):
        q = q_ref[j * Q_BLOCK:(j + 1) * Q_BLOCK, :]
        k = k_all[j * Q_BLOCK:(j + 2) * Q_BLOCK]
        out = []
        for h in range(HEADS_PER_GROUP):
            qh = jnp.where(heads[h], q, jnp.zeros_like(q))
            s = lax.dot_general(qh, k, (((1,), (1,)), ((), ())), preferred_element_type=f32)
            s = s * SCALE + bias_ref[h]
            out.append(s + first if j == 0 else s)
        return out

    def finish(j, ss):
        v = v_all[j * Q_BLOCK:(j + 2) * Q_BLOCK]
        ms = [jnp.max(s, axis=-1, keepdims=True) for s in ss]
        ps = [jnp.exp(s - m) for s, m in zip(ss, ms)]
        ls = [jnp.sum(p, axis=-1, keepdims=True) for p in ps]
        o = jnp.zeros((Q_BLOCK, GROUP_WIDTH), f32)
        lse = jnp.zeros((Q_BLOCK, GROUP_WIDTH), f32)
        for h in range(HEADS_PER_GROUP):
            pv = jnp.dot(ps[h].astype(bf16), v, preferred_element_type=f32)
            o = jnp.where(heads[h], pv / ls[h], o)
            lse = jnp.where(heads[h], ms[h] + jnp.log(ls[h]), lse)
        o_ref[j * Q_BLOCK:(j + 1) * Q_BLOCK, :] = o
        lse_ref[j * Q_BLOCK:(j + 1) * Q_BLOCK, :] = lse

    n_blocks = ATTN_TILE // Q_BLOCK
    ss = scores(0)
    for j in range(n_blocks):
        ss_next = scores(j + 1) if j + 1 < n_blocks else None
        finish(j, ss)
        ss = ss_next


def _prompt_attn(q, k, v, bias, g):
    d, n, _ = q.shape
    q, k, v = (t.reshape(d * n, GROUP_WIDTH) for t in (q, k, v))
    per_tile = ATTN_TILE // Q_BLOCK
    cur = pl.BlockSpec((ATTN_TILE, GROUP_WIDTH), lambda t: (t, 0))
    prev = pl.BlockSpec((Q_BLOCK, GROUP_WIDTH), lambda t: (jnp.maximum(t * per_tile - 1, 0), 0))
    out = jax.ShapeDtypeStruct((d * n, GROUP_WIDTH), f32)
    o, lse = pl.pallas_call(
        functools.partial(_prompt_attn_body, tiles_per_seq=n // ATTN_TILE),
        grid=(d * n // ATTN_TILE,),
        in_specs=[cur, prev, cur, prev, cur, _const_spec(bias.shape)],
        out_specs=(cur, cur),
        out_shape=(out, out),
        compiler_params=pltpu.CompilerParams(dimension_semantics=("arbitrary",)),
        name=f"prompt_attn_{g}",
    )(q, k, k, v, v, bias)
    return o.reshape(d, n, GROUP_WIDTH), lse.reshape(d, n, GROUP_WIDTH)


def _prompt_out_body(o0, o1, o2, l0, l1, l2, gas, gm, mb, x, woa, wo, npost, y, so1, so2, sl1, sl2):
    tm = x.shape[0]
    for src, dst, d in ((o1, so1, GROUPS[1][1]), (l1, sl1, GROUPS[1][1]),
                        (o2, so2, GROUPS[2][1]), (l2, sl2, GROUPS[2][1])):
        for c in range(d):
            for j in range(GROUP_WIDTH // LANES):
                dst[j, pl.ds(c, tm // d, stride=d), :] = src[c, :, j * LANES:(j + 1) * LANES]
    rows = lambda s: jnp.concatenate([s[j] for j in range(GROUP_WIDTH // LANES)], axis=1)
    a = _combine((o0[0], rows(so1), rows(so2)), (l0[0], rows(sl1), rows(sl2)))
    y[...] = _tail(a, gas[...], gm[...], mb[...], x[...], woa[...], wo[...], npost[...])


def _prompt_out(os, ls, gas, gm, mb, x, w_out_a, w_o, norm_post):
    n = x.shape[0]
    tm = ROW_TILE
    row = lambda w: pl.BlockSpec((tm, w), lambda i: (i, 0))
    grp = lambda d: pl.BlockSpec((d, tm // d, GROUP_WIDTH), lambda i: (0, i, 0))
    gspecs = [grp(d) for _, d in GROUPS]
    return pl.pallas_call(
        _prompt_out_body,
        grid=(n // tm,),
        in_specs=gspecs + gspecs + [row(GROUP_WIDTH), row(D_MODEL), row(D_MODEL), row(D_MODEL),
                                    _const_spec(w_out_a.shape), _const_spec(w_o.shape),
                                    _const_spec(norm_post.shape)],
        out_specs=row(D_MODEL),
        out_shape=jax.ShapeDtypeStruct((n, D_MODEL), f32),
        scratch_shapes=[pltpu.VMEM((GROUP_WIDTH // LANES, tm, LANES), f32)] * 4,
        compiler_params=pltpu.CompilerParams(dimension_semantics=("arbitrary",),
                                             vmem_limit_bytes=VMEM_LIMIT),
        name="prompt_out",
    )(*os, *ls, gas, gm, mb, x, w_out_a, w_o, norm_post)


def _sample_in_body(x_ref, np_ref, w_ref, st_ref, cw_ref, cb_ref, lg_ref, lb_ref, wob_ref,
                    qt, kt, vt, gas, gm, mb, st_out):
    h = _rms(x_ref[...], np_ref[...]).astype(bf16)
    proj = jnp.dot(h, w_ref[...], preferred_element_type=f32)
    qt[...] = proj[:, 0:ATTN_WIDTH].T
    kt[...] = proj[:, ATTN_WIDTH:2 * ATTN_WIDTH].T
    vt[...] = proj[:, 2 * ATTN_WIDTH:C_QKV].T
    gas[...] = _silu(proj[:, C_QKV:C_GA])
    gm[...] = _sigmoid(proj[:, C_GB:C_MA])
    glu = proj[:, C_GA:C_GLU]
    u = glu[:, :CONV_WIDTH] * _sigmoid(glu[:, CONV_WIDTH:])
    hist = CONV_KERNEL - 1
    c = cb_ref[...] + cw_ref[hist:hist + 1, :] * u
    for t in range(hist):
        c = c + cw_ref[t:t + 1, :] * st_ref[t]
    for t in range(hist - 1):
        st_out[t] = st_ref[t + 1]
    st_out[hist - 1] = u
    mb[...] = _conv_gate(c, proj[:, C_GLU:C_GB], lg_ref[...], lb_ref[...], wob_ref[...],
                         proj[:, C_MA:C_MB])


def _sample_in(x, norm_pre, w_in, state, conv_w, conv_b, ln_g, ln_b, w_out_b):
    b = x.shape[0]
    t = jax.ShapeDtypeStruct((ATTN_WIDTH, b), f32)
    return pl.pallas_call(
        _sample_in_body,
        out_shape=(t, t, t,
                   jax.ShapeDtypeStruct((b, GROUP_WIDTH), f32),
                   jax.ShapeDtypeStruct((b, D_MODEL), f32),
                   jax.ShapeDtypeStruct((b, D_MODEL), f32),
                   jax.ShapeDtypeStruct(state.shape, f32)),
        compiler_params=pltpu.CompilerParams(vmem_limit_bytes=VMEM_LIMIT),
        name="sample_in",
    )(x, norm_pre, w_in, state, conv_w, conv_b, ln_g, ln_b, w_out_b)


def _sample_out_body(ot, lt, gas, gm, mb, x, woa, wo, npost, y):
    nb = x.shape[0]
    os = [ot[g * HEADS_PER_GROUP:(g + 1) * HEADS_PER_GROUP] for g in range(N_GROUPS)]
    ls = [lt[g * HEADS_PER_GROUP:(g + 1) * HEADS_PER_GROUP] for g in range(N_GROUPS)]
    a = _combine(os, ls).reshape(GROUP_WIDTH, nb).T
    y[...] = _tail(a, gas[...], gm[...], mb[...], x[...], woa[...], wo[...], npost[...])


def _sample_out(ot, lt, gas, gm, mb, x, w_out_a, w_o, norm_post):
    return pl.pallas_call(
        _sample_out_body,
        out_shape=jax.ShapeDtypeStruct(x.shape, f32),
        name="sample_out",
    )(ot, lt, gas, gm, mb, x, w_out_a, w_o, norm_post)


def _t5_buckets(dist):
    n = np.asarray(dist, dtype=np.int64)
    large = MAX_EXACT + (np.log(np.maximum(n, 1) / MAX_EXACT) / np.log(REL_MAX_DIST / MAX_EXACT)
                         * (N_BUCKETS - MAX_EXACT)).astype(np.int64)
    large = np.minimum(large, N_BUCKETS - 1)
    return np.where(n < MAX_EXACT, n, large).astype(np.int32)


def _bias_rows(rel_bias, g, buckets, valid):
    onehot = (jnp.asarray(buckets)[..., None] == jnp.arange(N_BUCKETS, dtype=jnp.int32)).astype(f32)
    cols = rel_bias[:, g * HEADS_PER_GROUP:(g + 1) * HEADS_PER_GROUP]
    tab = jnp.einsum('...b,bh->h...', onehot, cols, precision=lax.Precision.HIGHEST)
    return jnp.where(jnp.asarray(valid)[None], tab, NEG)


def _prompt_bias(rel_bias, g):
    _, dil = GROUPS[g]
    j = np.arange(Q_BLOCK)[:, None] + Q_BLOCK - np.arange(2 * Q_BLOCK)[None, :]
    valid = (j >= 0) & (j <= KEYS_PER_QUERY)
    return _bias_rows(rel_bias, g, _t5_buckets(dil * np.clip(j, 0, KEYS_PER_QUERY)), valid)


def _sample_bias(rel_bias, w_buf):
    dist = w_buf - np.arange(w_buf)
    rows = []
    for g, (window, dil) in enumerate(GROUPS):
        valid = (dist % dil == 0) & (dist <= window)
        rows.append(_bias_rows(rel_bias, g, _t5_buckets(dist), valid))
    bias = jnp.concatenate(rows, axis=0)[:, None, :]
    bias0 = jnp.broadcast_to(rel_bias[int(_t5_buckets(0)), :][:, None, None], (N_HEADS, 1, 128))
    return bias, bias0


def kernel(x_prompt, x_sample, cache_k, cache_v, state_conv, rel_bias, norm_pre, w_in, w_out_a, conv_w,
           conv_b, ln_g, ln_b, w_out_b, w_o, norm_post):
    assert x_prompt.shape[0] == 1 and cache_k.shape[0] == 1 and x_sample.shape[1] == 1
    n = x_prompt.shape[1]
    nb, w_buf = cache_k.shape[1], cache_k.shape[2]
    keep = min(GROUPS[-1][0], n)
    w_in_b, woa_b, wob_b, wo_b = (w[0].astype(bf16) for w in (w_in, w_out_a, w_out_b, w_o))
    cw, cb, lg, lb = conv_w[0], conv_b, ln_g, ln_b
    xp = x_prompt[0]

    xs = x_sample[:, 0, :]
    state = jnp.transpose(state_conv[0], (1, 0, 2))
    qt, kt, vt, gas_s, gm_s, mb_s, state_new = _sample_in(xs, norm_pre, w_in_b, state, cw, cb, lg, lb, wob_b)
    kc = jnp.transpose(cache_k[0], (0, 2, 3, 1))
    vc = jnp.transpose(cache_v[0], (0, 2, 3, 1))
    bias, bias0 = _sample_bias(rel_bias, w_buf)
    heads = lambda t: t.reshape(N_HEADS, HEAD_DIM, nb)

    (q0, k0, v0, q1, k1, v1, q2, k2, v2, klast, vlast, gas, gm, mb, ulast, ot, lt, ko, vo) = _prompt_in(
        xp, norm_pre, w_in_b, cw, cb, lg, lb, wob_b, keep, heads(qt), heads(kt), heads(vt), bias, bias0, kc, vc)
    os, ls = [], []
    for g, (q, k, v) in enumerate(((q0, k0, v0), (q1, k1, v1), (q2, k2, v2))):
        o, l = _prompt_attn(q, k, v, _prompt_bias(rel_bias, g), g)
        os.append(o)
        ls.append(l)
    y_prompt = _prompt_out(os, ls, gas, gm, mb, xp, woa_b, wo_b, norm_post)
    y_sample = _sample_out(ot, lt, gas_s, gm_s, mb_s, xs, woa_b, wo_b, norm_post)

    back = lambda t: jnp.transpose(t, (0, 3, 1, 2))[None]
    return (y_prompt[None], y_sample[:, None, :],
            klast.reshape(1, 1, keep, N_HEADS, HEAD_DIM), vlast.reshape(1, 1, keep, N_HEADS, HEAD_DIM),
            ulast[CARRY_ROWS - (CONV_KERNEL - 1):][None, None],
            back(ko), back(vo), jnp.transpose(state_new, (1, 0, 2))[None])
```

```python
import functools

import numpy as np
import jax
import jax.numpy as jnp
from jax import lax
from jax.experimental import pallas as pl
from jax.experimental.pallas import tpu as pltpu

D_MODEL = 1024
GROUPS = ((128, 1), (512, 4), (2048, 16))
N_GROUPS = len(GROUPS)
HEADS_PER_GROUP = 4
N_HEADS = N_GROUPS * HEADS_PER_GROUP
HEAD_DIM = 64
GROUP_WIDTH = HEADS_PER_GROUP * HEAD_DIM
ATTN_WIDTH = N_HEADS * HEAD_DIM
KEYS_PER_QUERY = 128
N_BUCKETS = 32
MAX_EXACT = 16
REL_MAX_DIST = 2048
CONV_WIDTH = D_MODEL // 2
CONV_KERNEL = 31
EPS = 1e-6
NEG = -1e30
SCALE = HEAD_DIM ** -0.5

C_QKV = 3 * ATTN_WIDTH
C_GA = C_QKV + GROUP_WIDTH
C_GLU = C_GA + 2 * CONV_WIDTH
C_GB = C_GLU + CONV_WIDTH
C_MA = C_GB + D_MODEL
C_MB = C_MA + D_MODEL

IN_TILE = 256
ROW_TILE = 512
Q_BLOCK = 128
ATTN_TILE = 512
CACHE_HEADS = 2
CACHE_SLOTS = 4
LANES = 128
SUBLANES = 8
CARRY_ROWS = 32
VMEM_LIMIT = 52 * 1024 * 1024

f32 = jnp.float32
bf16 = jnp.bfloat16


def _rms(x, g):
    return x * lax.rsqrt(jnp.mean(x * x, axis=-1, keepdims=True) + EPS) * g


def _sigmoid(x):
    return 1.0 / (1.0 + jnp.exp(-x))


def _silu(x):
    return x * _sigmoid(x)


def _layernorm(c, g, b):
    mu = jnp.mean(c, axis=-1, keepdims=True)
    var = jnp.mean(jnp.square(c - mu), axis=-1, keepdims=True)
    return (c - mu) * lax.rsqrt(var + EPS) * g + b


def _mm(a, w):
    return jnp.dot(a.astype(bf16), w, preferred_element_type=f32)


def _conv_gate(c, gb, lg, lb, wob, mbp):
    c = _silu(_layernorm(c, lg, lb)) * _silu(gb)
    return _sigmoid(mbp) * _mm(c, wob)


def _tail(a, gas, gm, mb, x, woa, wo, npost):
    y_a = _mm(a * gas, woa)
    merged = gm * y_a + mb
    return x + _rms(_mm(merged, wo), npost)


def _combine(os, ls):
    m = jnp.maximum(jnp.maximum(ls[0], ls[1]), ls[2])
    es = [jnp.exp(l - m) for l in ls]
    den = es[0] + es[1] + es[2]
    return (es[0] * os[0] + es[1] * os[1] + es[2] * os[2]) / den


def _cache_attend_shift(kc_ref, vc_ref, ko_ref, vo_ref, hs, seq, qt_ref, kt_ref, vt_ref, bias_ref, bias0_ref,
                        o_ref, lse_ref):
    hb, hd, w = kc_ref.shape
    window = GROUPS[hs.start // HEADS_PER_GROUP][0]
    lo = max(w - window, 0) // LANES * LANES
    sel = lax.broadcasted_iota(jnp.int32, (1, 1, qt_ref.shape[-1]), 2) == seq
    pick = lambda ref: jnp.sum(jnp.where(sel, ref[hs], 0.0), axis=2, keepdims=True)
    qc, kn, vn = pick(qt_ref), pick(kt_ref), pick(vt_ref)
    s8 = jnp.zeros((hb, SUBLANES, w - lo), f32)
    for r in range(0, hd, SUBLANES):
        s8 = s8 + kc_ref[:, r:r + SUBLANES, lo:] * qc[:, r:r + SUBLANES, :]
    s = jnp.sum(s8, axis=1, keepdims=True) * SCALE + bias_ref[hs][:, :, lo:]
    s_new = jnp.sum(kn * qc, axis=1, keepdims=True) * SCALE + bias0_ref[hs][:, :, 0:1]
    m = jnp.maximum(jnp.max(s, axis=2, keepdims=True), s_new)
    p = jnp.exp(s - m)
    p_new = jnp.exp(s_new - m)
    l = jnp.sum(p, axis=2, keepdims=True) + p_new
    acc = jnp.zeros((hb, hd, LANES), f32)
    for c in range(0, w - lo, LANES):
        acc = acc + vc_ref[:, :, lo + c:lo + c + LANES] * p[:, :, c:c + LANES]
    o = (jnp.sum(acc, axis=2, keepdims=True) + vn * p_new) / l
    o_ref[hs] = jnp.where(sel, o, o_ref[hs])
    lse_ref[hs] = jnp.where(sel, m + jnp.log(l), lse_ref[hs])
    newest = lax.broadcasted_iota(jnp.int32, (1, 1, w), 2) == w - 1
    ko_ref[...] = jnp.where(newest, kn, pltpu.roll(kc_ref[...], w - 1, axis=2))
    vo_ref[...] = jnp.where(newest, vn, pltpu.roll(vc_ref[...], w - 1, axis=2))


def _prompt_in_body(x_ref, np_ref, w_ref, cw_ref, cb_ref, lg_ref, lb_ref, wob_ref,
                    qt_ref, kt_ref, vt_ref, bias_ref, bias0_ref, kc_hbm, vc_hbm,
                    q0, k0, v0, q1, k1, v1, q2, k2, v2, klast, vlast, gas, gm, mb, ulast,
                    ot_ref, lt_ref, ko_hbm, vo_hbm,
                    qkv_s, ubuf, cacc, kbuf, vbuf, kobuf, vobuf, sem_in, sem_out, *, per_tile):
    i = pl.program_id(0)
    nt = pl.num_programs(0)
    tm = x_ref.shape[0]
    slots, hb = kbuf.shape[0], kbuf.shape[1]
    per_seq = N_HEADS // hb
    ahead = slots - 1

    def block(t, j):
        return t * (per_tile // per_seq) + j // per_seq, (j % per_seq) * hb, j % slots

    def in_copies(t, j):
        seq, h0, slot = block(t, j)
        return (pltpu.make_async_copy(kc_hbm.at[seq, pl.ds(h0, hb)], kbuf.at[slot], sem_in.at[slot, 0]),
                pltpu.make_async_copy(vc_hbm.at[seq, pl.ds(h0, hb)], vbuf.at[slot], sem_in.at[slot, 1]))

    def out_copies(t, j):
        seq, h0, slot = block(t, j)
        return (pltpu.make_async_copy(kobuf.at[slot], ko_hbm.at[seq, pl.ds(h0, hb)], sem_out.at[slot, 0]),
                pltpu.make_async_copy(vobuf.at[slot], vo_hbm.at[seq, pl.ds(h0, hb)], sem_out.at[slot, 1]))

    def start(copies, priority):
        for cp in copies:
            cp.start(priority=priority)

    def wait(copies):
        for cp in copies:
            cp.wait()

    fetch = functools.partial(start, priority=0)
    write_back = functools.partial(start, priority=1)

    @pl.when(i == 0)
    def _():
        ot_ref[...] = jnp.zeros(ot_ref.shape, f32)
        lt_ref[...] = jnp.zeros(lt_ref.shape, f32)
        ubuf[0:CARRY_ROWS, :] = jnp.zeros((CARRY_ROWS, CONV_WIDTH), f32)
        for j in range(ahead):
            fetch(in_copies(i, j))

    def cache_block(j):
        seq, h0, slot = block(i, j)
        wait(in_copies(i, j))
        if j + ahead < per_tile:
            fetch(in_copies(i, j + ahead))
        else:
            @pl.when(i + 1 < nt)
            def _():
                fetch(in_copies(i + 1, j + ahead - per_tile))
        if j >= slots:
            wait(out_copies(i, j - slots))
        else:
            @pl.when(i > 0)
            def _():
                wait(out_copies(i - 1, per_tile - slots + j))
        _cache_attend_shift(kbuf.at[slot], vbuf.at[slot], kobuf.at[slot], vobuf.at[slot], slice(h0, h0 + hb), seq,
                            qt_ref, kt_ref, vt_ref, bias_ref, bias0_ref, ot_ref, lt_ref)
        if j < len(work):
            work[j]()
        write_back(out_copies(i, j))

    h = _rms(x_ref[...], np_ref[...]).astype(bf16)

    def project_qkv():
        qkv = jnp.dot(h, w_ref[:, 0:C_QKV], preferred_element_type=f32)
        klast[...] = qkv[:, ATTN_WIDTH:2 * ATTN_WIDTH]
        vlast[...] = qkv[:, 2 * ATTN_WIDTH:3 * ATTN_WIDTH]
        for j in range(C_QKV // LANES):
            qkv_s[j] = qkv[:, j * LANES:(j + 1) * LANES]

    def regroup_qkv(groups):
        outs = ((q0, k0, v0), (q1, k1, v1), (q2, k2, v2))
        for g in groups:
            dil = GROUPS[g][1]
            for part in range(3):
                col = part * ATTN_WIDTH + g * GROUP_WIDTH
                for c in range(dil):
                    for j in range(GROUP_WIDTH // LANES):
                        rows = qkv_s[col // LANES + j, pl.ds(c, tm // dil, stride=dil), :]
                        outs[g][part][c, :, j * LANES:(j + 1) * LANES] = rows.astype(bf16)

    def gates():
        gas[...] = _silu(jnp.dot(h, w_ref[:, C_QKV:C_GA], preferred_element_type=f32))
        gm[...] = _sigmoid(jnp.dot(h, w_ref[:, C_GB:C_MA], preferred_element_type=f32))

    def glu_unit():
        glu = jnp.dot(h, w_ref[:, C_GA:C_GLU], preferred_element_type=f32)
        ubuf[CARRY_ROWS:CARRY_ROWS + tm, :] = glu[:, :CONV_WIDTH] * _sigmoid(glu[:, CONV_WIDTH:])
        cacc[...] = jnp.zeros((tm, CONV_WIDTH), f32) + cb_ref[...]

    off = CARRY_ROWS - (CONV_KERNEL - 1)

    def conv_taps(shifts):
        c = cacc[...]
        for s in shifts:
            taps = [t for t in range(CONV_KERNEL) if (off + t) % SUBLANES == s]
            span = max(off + t - s for t in taps) + tm
            shifted = ubuf[s:s + span, :]
            for t in taps:
                c = c + cw_ref[t:t + 1, :] * shifted[off + t - s:off + t - s + tm]
        cacc[...] = c

    def conv_gate():
        last = ubuf[tm:tm + CARRY_ROWS, :]
        ubuf[0:CARRY_ROWS, :] = last
        ulast[...] = last
        gb = jnp.dot(h, w_ref[:, C_GLU:C_GB], preferred_element_type=f32)
        mbp = jnp.dot(h, w_ref[:, C_MA:C_MB], preferred_element_type=f32)
        mb[...] = _conv_gate(cacc[...], gb, lg_ref[...], lb_ref[...], wob_ref[...], mbp)

    work = [project_qkv, functools.partial(regroup_qkv, (0, 1)), functools.partial(regroup_qkv, (2,)),
            gates, glu_unit]
    work += [functools.partial(conv_taps, range(s, s + 2)) for s in range(0, SUBLANES, 2)]
    work += [conv_gate]
    assert len(work) <= per_tile
    for j in range(per_tile):
        cache_block(j)

    @pl.when(i == nt - 1)
    def _():
        for j in range(per_tile - slots, per_tile):
            wait(out_copies(i, j))


def _const_spec(shape):
    nd = len(shape)
    return pl.BlockSpec(shape, lambda *_: (0,) * nd, pipeline_mode=pl.Buffered(1))


def _prompt_in(x, norm_pre, w_in, conv_w, conv_b, ln_g, ln_b, w_out_b, keep, qt, kt, vt, bias, bias0, kc, vc):
    n = x.shape[0]
    tm = IN_TILE
    nt = n // tm
    keep_tiles = keep // tm
    nb, nh, hd, w = kc.shape
    per_seq = nh // CACHE_HEADS
    per_tile = nb * per_seq // nt
    assert per_tile * nt == nb * per_seq and per_tile % per_seq == 0 and per_tile % CACHE_SLOTS == 0
    row = lambda w: pl.BlockSpec((tm, w), lambda i: (i, 0))
    grp = lambda d: pl.BlockSpec((d, tm // d, GROUP_WIDTH), lambda i: (0, i, 0))
    last = pl.BlockSpec((tm, ATTN_WIDTH), lambda i: (jnp.maximum(i - (nt - keep_tiles), 0), 0))
    hbm = pl.BlockSpec(memory_space=pl.ANY)
    out_shape, out_specs = [], []
    for _, d in GROUPS:
        for _ in range(3):
            out_shape.append(jax.ShapeDtypeStruct((d, n // d, GROUP_WIDTH), bf16))
            out_specs.append(grp(d))
    out_shape += [jax.ShapeDtypeStruct((keep, ATTN_WIDTH), f32)] * 2
    out_specs += [last, last]
    out_shape += [jax.ShapeDtypeStruct((n, GROUP_WIDTH), f32),
                  jax.ShapeDtypeStruct((n, D_MODEL), f32),
                  jax.ShapeDtypeStruct((n, D_MODEL), f32),
                  jax.ShapeDtypeStruct((CARRY_ROWS, CONV_WIDTH), f32)]
    out_specs += [row(GROUP_WIDTH), row(D_MODEL), row(D_MODEL),
                  pl.BlockSpec((CARRY_ROWS, CONV_WIDTH), lambda i: (0, 0))]
    out_shape += [jax.ShapeDtypeStruct((nh, hd, nb), f32), jax.ShapeDtypeStruct((nh, 1, nb), f32),
                  jax.ShapeDtypeStruct(kc.shape, f32), jax.ShapeDtypeStruct(vc.shape, f32)]
    out_specs += [pl.BlockSpec((nh, hd, nb), lambda i: (0, 0, 0)), pl.BlockSpec((nh, 1, nb), lambda i: (0, 0, 0)),
                  hbm, hbm]
    cache_buf = pltpu.VMEM((CACHE_SLOTS, CACHE_HEADS, hd, w), f32)
    sems = pltpu.SemaphoreType.DMA((CACHE_SLOTS, 2))
    return pl.pallas_call(
        functools.partial(_prompt_in_body, per_tile=per_tile),
        grid=(nt,),
        in_specs=[row(D_MODEL), _const_spec(norm_pre.shape), _const_spec(w_in.shape),
                  _const_spec(conv_w.shape), _const_spec(conv_b.shape), _const_spec(ln_g.shape),
                  _const_spec(ln_b.shape), _const_spec(w_out_b.shape),
                  _const_spec(qt.shape), _const_spec(kt.shape), _const_spec(vt.shape),
                  _const_spec(bias.shape), _const_spec(bias0.shape), hbm, hbm],
        out_specs=out_specs,
        out_shape=out_shape,
        scratch_shapes=[pltpu.VMEM((C_QKV // LANES, tm, LANES), f32),
                        pltpu.VMEM((tm + CARRY_ROWS, CONV_WIDTH), f32),
                        pltpu.VMEM((tm, CONV_WIDTH), f32),
                        cache_buf, cache_buf, cache_buf, cache_buf, sems, sems],
        compiler_params=pltpu.CompilerParams(dimension_semantics=("arbitrary",),
                                             vmem_limit_bytes=VMEM_LIMIT),
        name="prompt_in",
    )(x, norm_pre, w_in, conv_w, conv_b, ln_g, ln_b, w_out_b, qt, kt, vt, bias, bias0, kc, vc)


def _prompt_attn_body(q_ref, kp_ref, kc_ref, vp_ref, vc_ref, bias_ref, o_ref, lse_ref, *, tiles_per_seq):
    first_tile = pl.program_id(0) % tiles_per_seq == 0
    k_all = jnp.concatenate([kp_ref[...], kc_ref[...]], axis=0)
    v_all = jnp.concatenate([vp_ref[...], vc_ref[...]], axis=0)
    lane = lax.broadcasted_iota(jnp.int32, (1, GROUP_WIDTH), 1)
    key = lax.broadcasted_iota(jnp.int32, (1, 2 * Q_BLOCK), 1)
    first = jnp.where((key < Q_BLOCK) & first_tile, NEG, 0.0).astype(f32)
    heads = [(lane >= h * HEAD_DIM) & (lane < (h + 1) * HEAD_DIM) for h in range(HEADS_PER_GROUP)]

    def scores(j):
        q = q_ref[j * Q_BLOCK:(j + 1) * Q_BLOCK, :]
        k = k_all[j * Q_BLOCK:(j + 2) * Q_BLOCK]
        out = []
        for h in range(HEADS_PER_GROUP):
            qh = jnp.where(heads[h], q, jnp.zeros_like(q))
            s = lax.dot_general(qh, k, (((1,), (1,)), ((), ())), preferred_element_type=f32)
            s = s * SCALE + bias_ref[h]
            out.append(s + first if j == 0 else s)
        return out

    def finish(j, ss):
        v = v_all[j * Q_BLOCK:(j + 2) * Q_BLOCK]
        ms = [jnp.max(s, axis=-1, keepdims=True) for s in ss]
        ps = [jnp.exp(s - m) for s, m in zip(ss, ms)]
        ls = [jnp.sum(p, axis=-1, keepdims=True) for p in ps]
        o = jnp.zeros((Q_BLOCK, GROUP_WIDTH), f32)
        lse = jnp.zeros((Q_BLOCK, GROUP_WIDTH), f32)
        for h in range(HEADS_PER_GROUP):
            pv = jnp.dot(ps[h].astype(bf16), v, preferred_element_type=f32)
            o = jnp.where(heads[h], pv / ls[h], o)
            lse = jnp.where(heads[h], ms[h] + jnp.log(ls[h]), lse)
        o_ref[j * Q_BLOCK:(j + 1) * Q_BLOCK, :] = o
        lse_ref[j * Q_BLOCK:(j + 1) * Q_BLOCK, :] = lse

    n_blocks = ATTN_TILE // Q_BLOCK
    ss = scores(0)
    for j in range(n_blocks):
        ss_next = scores(j + 1) if j + 1 < n_blocks else None
        finish(j, ss)
        ss = ss_next


def _prompt_attn(q, k, v, bias, g):
    d, n, _ = q.shape
    q, k, v = (t.reshape(d * n, GROUP_WIDTH) for t in (q, k, v))
    per_tile = ATTN_TILE // Q_BLOCK
    cur = pl.BlockSpec((ATTN_TILE, GROUP_WIDTH), lambda t: (t, 0))
    prev = pl.BlockSpec((Q_BLOCK, GROUP_WIDTH), lambda t: (jnp.maximum(t * per_tile - 1, 0), 0))
    out = jax.ShapeDtypeStruct((d * n, GROUP_WIDTH), f32)
    o, lse = pl.pallas_call(
        functools.partial(_prompt_attn_body, tiles_per_seq=n // ATTN_TILE),
        grid=(d * n // ATTN_TILE,),
        in_specs=[cur, prev, cur, prev, cur, _const_spec(bias.shape)],
        out_specs=(cur, cur),
        out_shape=(out, out),
        compiler_params=pltpu.CompilerParams(dimension_semantics=("arbitrary",)),
        name=f"prompt_attn_{g}",
    )(q, k, k, v, v, bias)
    return o.reshape(d, n, GROUP_WIDTH), lse.reshape(d, n, GROUP_WIDTH)


def _prompt_out_body(o0, o1, o2, l0, l1, l2, gas, gm, mb, x, woa, wo, npost, y, so1, so2, sl1, sl2):
    tm = x.shape[0]
    for src, dst, d in ((o1, so1, GROUPS[1][1]), (l1, sl1, GROUPS[1][1]),
                        (o2, so2, GROUPS[2][1]), (l2, sl2, GROUPS[2][1])):
        for c in range(d):
            for j in range(GROUP_WIDTH // LANES):
                dst[j, pl.ds(c, tm // d, stride=d), :] = src[c, :, j * LANES:(j + 1) * LANES]
    rows = lambda s: jnp.concatenate([s[j] for j in range(GROUP_WIDTH // LANES)], axis=1)
    a = _combine((o0[0], rows(so1), rows(so2)), (l0[0], rows(sl1), rows(sl2)))
    y[...] = _tail(a, gas[...], gm[...], mb[...], x[...], woa[...], wo[...], npost[...])


def _prompt_out(os, ls, gas, gm, mb, x, w_out_a, w_o, norm_post):
    n = x.shape[0]
    tm = ROW_TILE
    row = lambda w: pl.BlockSpec((tm, w), lambda i: (i, 0))
    grp = lambda d: pl.BlockSpec((d, tm // d, GROUP_WIDTH), lambda i: (0, i, 0))
    gspecs = [grp(d) for _, d in GROUPS]
    return pl.pallas_call(
        _prompt_out_body,
        grid=(n // tm,),
        in_specs=gspecs + gspecs + [row(GROUP_WIDTH), row(D_MODEL), row(D_MODEL), row(D_MODEL),
                                    _const_spec(w_out_a.shape), _const_spec(w_o.shape),
                                    _const_spec(norm_post.shape)],
        out_specs=row(D_MODEL),
        out_shape=jax.ShapeDtypeStruct((n, D_MODEL), f32),
        scratch_shapes=[pltpu.VMEM((GROUP_WIDTH // LANES, tm, LANES), f32)] * 4,
        compiler_params=pltpu.CompilerParams(dimension_semantics=("arbitrary",),
                                             vmem_limit_bytes=VMEM_LIMIT),
        name="prompt_out",
    )(*os, *ls, gas, gm, mb, x, w_out_a, w_o, norm_post)


def _sample_in_body(x_ref, np_ref, w_ref, st_ref, cw_ref, cb_ref, lg_ref, lb_ref, wob_ref,
                    qt, kt, vt, gas, gm, mb, st_out):
    h = _rms(x_ref[...], np_ref[...]).astype(bf16)
    proj = jnp.dot(h, w_ref[...], preferred_element_type=f32)
    qt[...] = proj[:, 0:ATTN_WIDTH].T
    kt[...] = proj[:, ATTN_WIDTH:2 * ATTN_WIDTH].T
    vt[...] = proj[:, 2 * ATTN_WIDTH:C_QKV].T
    gas[...] = _silu(proj[:, C_QKV:C_GA])
    gm[...] = _sigmoid(proj[:, C_GB:C_MA])
    glu = proj[:, C_GA:C_GLU]
    u = glu[:, :CONV_WIDTH] * _sigmoid(glu[:, CONV_WIDTH:])
    hist = CONV_KERNEL - 1
    c = cb_ref[...] + cw_ref[hist:hist + 1, :] * u
    for t in range(hist):
        c = c + cw_ref[t:t + 1, :] * st_ref[t]
    for t in range(hist - 1):
        st_out[t] = st_ref[t + 1]
    st_out[hist - 1] = u
    mb[...] = _conv_gate(c, proj[:, C_GLU:C_GB], lg_ref[...], lb_ref[...], wob_ref[...],
                         proj[:, C_MA:C_MB])


def _sample_in(x, norm_pre, w_in, state, conv_w, conv_b, ln_g, ln_b, w_out_b):
    b = x.shape[0]
    t = jax.ShapeDtypeStruct((ATTN_WIDTH, b), f32)
    return pl.pallas_call(
        _sample_in_body,
        out_shape=(t, t, t,
                   jax.ShapeDtypeStruct((b, GROUP_WIDTH), f32),
                   jax.ShapeDtypeStruct((b, D_MODEL), f32),
                   jax.ShapeDtypeStruct((b, D_MODEL), f32),
                   jax.ShapeDtypeStruct(state.shape, f32)),
        compiler_params=pltpu.CompilerParams(vmem_limit_bytes=VMEM_LIMIT),
        name="sample_in",
    )(x, norm_pre, w_in, state, conv_w, conv_b, ln_g, ln_b, w_out_b)


def _sample_out_body(ot, lt, gas, gm, mb, x, woa, wo, npost, y):
    nb = x.shape[0]
    os = [ot[g * HEADS_PER_GROUP:(g + 1) * HEADS_PER_GROUP] for g in range(N_GROUPS)]
    ls = [lt[g * HEADS_PER_GROUP:(g + 1) * HEADS_PER_GROUP] for g in range(N_GROUPS)]
    a = _combine(os, ls).reshape(GROUP_WIDTH, nb).T
    y[...] = _tail(a, gas[...], gm[...], mb[...], x[...], woa[...], wo[...], npost[...])


def _sample_out(ot, lt, gas, gm, mb, x, w_out_a, w_o, norm_post):
    return pl.pallas_call(
        _sample_out_body,
        out_shape=jax.ShapeDtypeStruct(x.shape, f32),
        name="sample_out",
    )(ot, lt, gas, gm, mb, x, w_out_a, w_o, norm_post)


def _t5_buckets(dist):
    n = np.asarray(dist, dtype=np.int64)
    large = MAX_EXACT + (np.log(np.maximum(n, 1) / MAX_EXACT) / np.log(REL_MAX_DIST / MAX_EXACT)
                         * (N_BUCKETS - MAX_EXACT)).astype(np.int64)
    large = np.minimum(large, N_BUCKETS - 1)
    return np.where(n < MAX_EXACT, n, large).astype(np.int32)


def _bias_rows(rel_bias, g, buckets, valid):
    onehot = (jnp.asarray(buckets)[..., None] == jnp.arange(N_BUCKETS, dtype=jnp.int32)).astype(f32)
    cols = rel_bias[:, g * HEADS_PER_GROUP:(g + 1) * HEADS_PER_GROUP]
    tab = jnp.einsum('...b,bh->h...', onehot, cols, precision=lax.Precision.HIGHEST)
    return jnp.where(jnp.asarray(valid)[None], tab, NEG)


def _prompt_bias(rel_bias, g):
    _, dil = GROUPS[g]
    j = np.arange(Q_BLOCK)[:, None] + Q_BLOCK - np.arange(2 * Q_BLOCK)[None, :]
    valid = (j >= 0) & (j <= KEYS_PER_QUERY)
    return _bias_rows(rel_bias, g, _t5_buckets(dil * np.clip(j, 0, KEYS_PER_QUERY)), valid)


def _sample_bias(rel_bias, w_buf):
    dist = w_buf - np.arange(w_buf)
    rows = []
    for g, (window, dil) in enumerate(GROUPS):
        valid = (dist % dil == 0) & (dist <= window)
        rows.append(_bias_rows(rel_bias, g, _t5_buckets(dist), valid))
    bias = jnp.concatenate(rows, axis=0)[:, None, :]
    bias0 = jnp.broadcast_to(rel_bias[int(_t5_buckets(0)), :][:, None, None], (N_HEADS, 1, 128))
    return bias, bias0


def kernel(x_prompt, x_sample, cache_k, cache_v, state_conv, rel_bias, norm_pre, w_in, w_out_a, conv_w,
           conv_b, ln_g, ln_b, w_out_b, w_o, norm_post):
    assert x_prompt.shape[0] == 1 and cache_k.shape[0] == 1 and x_sample.shape[1] == 1
    n = x_prompt.shape[1]
    nb, w_buf = cache_k.shape[1], cache_k.shape[2]
    keep = min(GROUPS[-1][0], n)
    w_in_b, woa_b, wob_b, wo_b = (w[0].astype(bf16) for w in (w_in, w_out_a, w_out_b, w_o))
    cw, cb, lg, lb = conv_w[0], conv_b, ln_g, ln_b
    xp = x_prompt[0]

    xs = x_sample[:, 0, :]
    state = jnp.transpose(state_conv[0], (1, 0, 2))
    qt, kt, vt, gas_s, gm_s, mb_s, state_new = _sample_in(xs, norm_pre, w_in_b, state, cw, cb, lg, lb, wob_b)
    kc = jnp.transpose(cache_k[0], (0, 2, 3, 1))
    vc = jnp.transpose(cache_v[0], (0, 2, 3, 1))
    bias, bias0 = _sample_bias(rel_bias, w_buf)
    heads = lambda t: t.reshape(N_HEADS, HEAD_DIM, nb)

    (q0, k0, v0, q1, k1, v1, q2, k2, v2, klast, vlast, gas, gm, mb, ulast, ot, lt, ko, vo) = _prompt_in(
        xp, norm_pre, w_in_b, cw, cb, lg, lb, wob_b, keep, heads(qt), heads(kt), heads(vt), bias, bias0, kc, vc)
    os, ls = [], []
    for g, (q, k, v) in enumerate(((q0, k0, v0), (q1, k1, v1), (q2, k2, v2))):
        o, l = _prompt_attn(q, k, v, _prompt_bias(rel_bias, g), g)
        os.append(o)
        ls.append(l)
    y_prompt = _prompt_out(os, ls, gas, gm, mb, xp, woa_b, wo_b, norm_post)
    y_sample = _sample_out(ot, lt, gas_s, gm_s, mb_s, xs, woa_b, wo_b, norm_post)

    back = lambda t: jnp.transpose(t, (0, 3, 1, 2))[None]
    return (y_prompt[None], y_sample[:, None, :],
            klast.reshape(1, 1, keep, N_HEADS, HEAD_DIM), vlast.reshape(1, 1, keep, N_HEADS, HEAD_DIM),
            ulast[CARRY_ROWS - (CONV_KERNEL - 1):][None, None],
            back(ko), back(vo), jnp.transpose(state_new, (1, 0, 2))[None])
```

```python
import functools

import numpy as np
import jax
import jax.numpy as jnp
from jax import lax
from jax.experimental import pallas as pl
from jax.experimental.pallas import tpu as pltpu

D_MODEL = 1024
GROUPS = ((128, 1), (512, 4), (2048, 16))
N_GROUPS = len(GROUPS)
HEADS_PER_GROUP = 4
N_HEADS = N_GROUPS * HEADS_PER_GROUP
HEAD_DIM = 64
GROUP_WIDTH = HEADS_PER_GROUP * HEAD_DIM
ATTN_WIDTH = N_HEADS * HEAD_DIM
KEYS_PER_QUERY = 128
N_BUCKETS = 32
MAX_EXACT = 16
REL_MAX_DIST = 2048
CONV_WIDTH = D_MODEL // 2
CONV_KERNEL = 31
EPS = 1e-6
NEG = -1e30
SCALE = HEAD_DIM ** -0.5

C_QKV = 3 * ATTN_WIDTH
C_GA = C_QKV + GROUP_WIDTH
C_GLU = C_GA + 2 * CONV_WIDTH
C_GB = C_GLU + CONV_WIDTH
C_MA = C_GB + D_MODEL
C_MB = C_MA + D_MODEL

IN_TILE = 256
ROW_TILE = 512
Q_BLOCK = 128
ATTN_TILE = 512
CACHE_HEADS = 2
CACHE_SLOTS = 4
LANES = 128
SUBLANES = 8
CARRY_ROWS = 32
VMEM_LIMIT = 52 * 1024 * 1024

f32 = jnp.float32
bf16 = jnp.bfloat16


def _rms(x, g):
    return x * lax.rsqrt(jnp.mean(x * x, axis=-1, keepdims=True) + EPS) * g


def _sigmoid(x):
    return 1.0 / (1.0 + jnp.exp(-x))


def _silu(x):
    return x * _sigmoid(x)


def _layernorm(c, g, b):
    mu = jnp.mean(c, axis=-1, keepdims=True)
    var = jnp.mean(jnp.square(c - mu), axis=-1, keepdims=True)
    return (c - mu) * lax.rsqrt(var + EPS) * g + b


def _mm(a, w):
    return jnp.dot(a.astype(bf16), w, preferred_element_type=f32)


def _conv_gate(c, gb, lg, lb, wob, mbp):
    c = _silu(_layernorm(c, lg, lb)) * _silu(gb)
    return _sigmoid(mbp) * _mm(c, wob)


def _tail(a, gas, gm, mb, x, woa, wo, npost):
    y_a = _mm(a * gas, woa)
    merged = gm * y_a + mb
    return x + _rms(_mm(merged, wo), npost)


def _combine(os, ls):
    m = jnp.maximum(jnp.maximum(ls[0], ls[1]), ls[2])
    es = [jnp.exp(l - m) for l in ls]
    den = es[0] + es[1] + es[2]
    return (es[0] * os[0] + es[1] * os[1] + es[2] * os[2]) / den


def _cache_attend_shift(kc_ref, vc_ref, ko_ref, vo_ref, hs, seq, qt_ref, kt_ref, vt_ref, bias_ref, bias0_ref,
                        o_ref, lse_ref):
    hb, hd, w = kc_ref.shape
    window = GROUPS[hs.start // HEADS_PER_GROUP][0]
    lo = max(w - window, 0) // LANES * LANES
    sel = lax.broadcasted_iota(jnp.int32, (1, 1, qt_ref.shape[-1]), 2) == seq
    pick = lambda ref: jnp.sum(jnp.where(sel, ref[hs], 0.0), axis=2, keepdims=True)
    qc, kn, vn = pick(qt_ref), pick(kt_ref), pick(vt_ref)
    s8 = jnp.zeros((hb, SUBLANES, w - lo), f32)
    for r in range(0, hd, SUBLANES):
        s8 = s8 + kc_ref[:, r:r + SUBLANES, lo:] * qc[:, r:r + SUBLANES, :]
    s = jnp.sum(s8, axis=1, keepdims=True) * SCALE + bias_ref[hs][:, :, lo:]
    s_new = jnp.sum(kn * qc, axis=1, keepdims=True) * SCALE + bias0_ref[hs][:, :, 0:1]
    m = jnp.maximum(jnp.max(s, axis=2, keepdims=True), s_new)
    p = jnp.exp(s - m)
    p_new = jnp.exp(s_new - m)
    l = jnp.sum(p, axis=2, keepdims=True) + p_new
    acc = jnp.zeros((hb, hd, LANES), f32)
    for c in range(0, w - lo, LANES):
        acc = acc + vc_ref[:, :, lo + c:lo + c + LANES] * p[:, :, c:c + LANES]
    o = (jnp.sum(acc, axis=2, keepdims=True) + vn * p_new) / l
    o_ref[hs] = jnp.where(sel, o, o_ref[hs])
    lse_ref[hs] = jnp.where(sel, m + jnp.log(l), lse_ref[hs])
    newest = lax.broadcasted_iota(jnp.int32, (1, 1, w), 2) == w - 1
    ko_ref[...] = jnp.where(newest, kn, pltpu.roll(kc_ref[...], w - 1, axis=2))
    vo_ref[...] = jnp.where(newest, vn, pltpu.roll(vc_ref[...], w - 1, axis=2))


def _prompt_in_body(x_ref, np_ref, w_ref, cw_ref, cb_ref, lg_ref, lb_ref, wob_ref,
                    qt_ref, kt_ref, vt_ref, bias_ref, bias0_ref, kc_hbm, vc_hbm,
                    q0, k0, v0, q1, k1, v1, q2, k2, v2, klast, vlast, gas, gm, mb, ulast,
                    ot_ref, lt_ref, ko_hbm, vo_hbm,
                    qkv_s, ubuf, cacc, kbuf, vbuf, kobuf, vobuf, sem_in, sem_out, *, per_tile):
    i = pl.program_id(0)
    nt = pl.num_programs(0)
    tm = x_ref.shape[0]
    slots, hb = kbuf.shape[0], kbuf.shape[1]
    per_seq = N_HEADS // hb
    ahead = slots - 1

    def block(t, j):
        return t * (per_tile // per_seq) + j // per_seq, (j % per_seq) * hb, j % slots

    def in_copies(t, j):
        seq, h0, slot = block(t, j)
        return (pltpu.make_async_copy(kc_hbm.at[seq, pl.ds(h0, hb)], kbuf.at[slot], sem_in.at[slot, 0]),
                pltpu.make_async_copy(vc_hbm.at[seq, pl.ds(h0, hb)], vbuf.at[slot], sem_in.at[slot, 1]))

    def out_copies(t, j):
        seq, h0, slot = block(t, j)
        return (pltpu.make_async_copy(kobuf.at[slot], ko_hbm.at[seq, pl.ds(h0, hb)], sem_out.at[slot, 0]),
                pltpu.make_async_copy(vobuf.at[slot], vo_hbm.at[seq, pl.ds(h0, hb)], sem_out.at[slot, 1]))

    def start(copies, priority):
        for cp in copies:
            cp.start(priority=priority)

    def wait(copies):
        for cp in copies:
            cp.wait()

    fetch = functools.partial(start, priority=0)
    write_back = functools.partial(start, priority=1)

    @pl.when(i == 0)
    def _():
        ot_ref[...] = jnp.zeros(ot_ref.shape, f32)
        lt_ref[...] = jnp.zeros(lt_ref.shape, f32)
        ubuf[0:CARRY_ROWS, :] = jnp.zeros((CARRY_ROWS, CONV_WIDTH), f32)
        for j in range(ahead):
            fetch(in_copies(i, j))

    def cache_block(j):
        seq, h0, slot = block(i, j)
        wait(in_copies(i, j))
        if j + ahead < per_tile:
            fetch(in_copies(i, j + ahead))
        else:
            @pl.when(i + 1 < nt)
            def _():
                fetch(in_copies(i + 1, j + ahead - per_tile))
        if j >= slots:
            wait(out_copies(i, j - slots))
        else:
            @pl.when(i > 0)
            def _():
                wait(out_copies(i - 1, per_tile - slots + j))
        _cache_attend_shift(kbuf.at[slot], vbuf.at[slot], kobuf.at[slot], vobuf.at[slot], slice(h0, h0 + hb), seq,
                            qt_ref, kt_ref, vt_ref, bias_ref, bias0_ref, ot_ref, lt_ref)
        if j < len(work):
            work[j]()
        write_back(out_copies(i, j))

    h = _rms(x_ref[...], np_ref[...]).astype(bf16)

    def project_qkv():
        qkv = jnp.dot(h, w_ref[:, 0:C_QKV], preferred_element_type=f32)
        klast[...] = qkv[:, ATTN_WIDTH:2 * ATTN_WIDTH]
        vlast[...] = qkv[:, 2 * ATTN_WIDTH:3 * ATTN_WIDTH]
        for j in range(C_QKV // LANES):
            qkv_s[j] = qkv[:, j * LANES:(j + 1) * LANES]

    def regroup_qkv(groups):
        outs = ((q0, k0, v0), (q1, k1, v1), (q2, k2, v2))
        for g in groups:
            dil = GROUPS[g][1]
            for part in range(3):
                col = part * ATTN_WIDTH + g * GROUP_WIDTH
                for c in range(dil):
                    for j in range(GROUP_WIDTH // LANES):
                        rows = qkv_s[col // LANES + j, pl.ds(c, tm // dil, stride=dil), :]
                        if part == 0:
                            rows = rows * SCALE
                        outs[g][part][c, :, j * LANES:(j + 1) * LANES] = rows.astype(bf16)

    def gates():
        gas[...] = _silu(jnp.dot(h, w_ref[:, C_QKV:C_GA], preferred_element_type=f32)).astype(gas.dtype)
        gm[...] = _sigmoid(jnp.dot(h, w_ref[:, C_GB:C_MA], preferred_element_type=f32)).astype(gm.dtype)

    def glu_unit():
        glu = jnp.dot(h, w_ref[:, C_GA:C_GLU], preferred_element_type=f32)
        ubuf[CARRY_ROWS:CARRY_ROWS + tm, :] = glu[:, :CONV_WIDTH] * _sigmoid(glu[:, CONV_WIDTH:])
        cacc[...] = jnp.zeros((tm, CONV_WIDTH), f32) + cb_ref[...]

    off = CARRY_ROWS - (CONV_KERNEL - 1)

    def conv_taps(shifts):
        c = cacc[...]
        for s in shifts:
            taps = [t for t in range(CONV_KERNEL) if (off + t) % SUBLANES == s]
            span = max(off + t - s for t in taps) + tm
            shifted = ubuf[s:s + span, :]
            for t in taps:
                c = c + cw_ref[t:t + 1, :] * shifted[off + t - s:off + t - s + tm]
        cacc[...] = c

    def conv_gate():
        last = ubuf[tm:tm + CARRY_ROWS, :]
        ubuf[0:CARRY_ROWS, :] = last
        ulast[...] = last
        gb = jnp.dot(h, w_ref[:, C_GLU:C_GB], preferred_element_type=f32)
        mbp = jnp.dot(h, w_ref[:, C_MA:C_MB], preferred_element_type=f32)
        mb[...] = _conv_gate(cacc[...], gb, lg_ref[...], lb_ref[...], wob_ref[...], mbp).astype(mb.dtype)

    work = [project_qkv, functools.partial(regroup_qkv, (0, 1)), functools.partial(regroup_qkv, (2,)),
            gates, glu_unit]
    work += [functools.partial(conv_taps, range(s, s + 2)) for s in range(0, SUBLANES, 2)]
    work += [conv_gate]
    assert len(work) <= per_tile
    for j in range(per_tile):
        cache_block(j)

    @pl.when(i == nt - 1)
    def _():
        for j in range(per_tile - slots, per_tile):
            wait(out_copies(i, j))


def _const_spec(shape):
    nd = len(shape)
    return pl.BlockSpec(shape, lambda *_: (0,) * nd, pipeline_mode=pl.Buffered(1))


def _prompt_in(x, norm_pre, w_in, conv_w, conv_b, ln_g, ln_b, w_out_b, keep, qt, kt, vt, bias, bias0, kc, vc):
    n = x.shape[0]
    tm = IN_TILE
    nt = n // tm
    keep_tiles = keep // tm
    nb, nh, hd, w = kc.shape
    per_seq = nh // CACHE_HEADS
    per_tile = nb * per_seq // nt
    assert per_tile * nt == nb * per_seq and per_tile % per_seq == 0 and per_tile % CACHE_SLOTS == 0
    row = lambda w: pl.BlockSpec((tm, w), lambda i: (i, 0))
    grp = lambda d: pl.BlockSpec((d, tm // d, GROUP_WIDTH), lambda i: (0, i, 0))
    last = pl.BlockSpec((tm, ATTN_WIDTH), lambda i: (jnp.maximum(i - (nt - keep_tiles), 0), 0))
    hbm = pl.BlockSpec(memory_space=pl.ANY)
    out_shape, out_specs = [], []
    for _, d in GROUPS:
        for _ in range(3):
            out_shape.append(jax.ShapeDtypeStruct((d, n // d, GROUP_WIDTH), bf16))
            out_specs.append(grp(d))
    out_shape += [jax.ShapeDtypeStruct((keep, ATTN_WIDTH), f32)] * 2
    out_specs += [last, last]
    out_shape += [jax.ShapeDtypeStruct((n, GROUP_WIDTH), bf16),
                  jax.ShapeDtypeStruct((n, D_MODEL), bf16),
                  jax.ShapeDtypeStruct((n, D_MODEL), bf16),
                  jax.ShapeDtypeStruct((CARRY_ROWS, CONV_WIDTH), f32)]
    out_specs += [row(GROUP_WIDTH), row(D_MODEL), row(D_MODEL),
                  pl.BlockSpec((CARRY_ROWS, CONV_WIDTH), lambda i: (0, 0))]
    out_shape += [jax.ShapeDtypeStruct((nh, hd, nb), f32), jax.ShapeDtypeStruct((nh, 1, nb), f32),
                  jax.ShapeDtypeStruct(kc.shape, f32), jax.ShapeDtypeStruct(vc.shape, f32)]
    out_specs += [pl.BlockSpec((nh, hd, nb), lambda i: (0, 0, 0)), pl.BlockSpec((nh, 1, nb), lambda i: (0, 0, 0)),
                  hbm, hbm]
    cache_buf = pltpu.VMEM((CACHE_SLOTS, CACHE_HEADS, hd, w), f32)
    sems = pltpu.SemaphoreType.DMA((CACHE_SLOTS, 2))
    return pl.pallas_call(
        functools.partial(_prompt_in_body, per_tile=per_tile),
        grid=(nt,),
        in_specs=[row(D_MODEL), _const_spec(norm_pre.shape), _const_spec(w_in.shape),
                  _const_spec(conv_w.shape), _const_spec(conv_b.shape), _const_spec(ln_g.shape),
                  _const_spec(ln_b.shape), _const_spec(w_out_b.shape),
                  _const_spec(qt.shape), _const_spec(kt.shape), _const_spec(vt.shape),
                  _const_spec(bias.shape), _const_spec(bias0.shape), hbm, hbm],
        out_specs=out_specs,
        out_shape=out_shape,
        scratch_shapes=[pltpu.VMEM((C_QKV // LANES, tm, LANES), f32),
                        pltpu.VMEM((tm + CARRY_ROWS, CONV_WIDTH), f32),
                        pltpu.VMEM((tm, CONV_WIDTH), f32),
                        cache_buf, cache_buf, cache_buf, cache_buf, sems, sems],
        compiler_params=pltpu.CompilerParams(dimension_semantics=("arbitrary",),
                                             vmem_limit_bytes=VMEM_LIMIT),
        name="prompt_in",
    )(x, norm_pre, w_in, conv_w, conv_b, ln_g, ln_b, w_out_b, qt, kt, vt, bias, bias0, kc, vc)


def _prompt_attn_body(q_ref, kp_ref, kc_ref, vp_ref, vc_ref, bias_ref, o_ref, lse_ref, *, tiles_per_seq):
    first_tile = pl.program_id(0) % tiles_per_seq == 0
    k_all = jnp.concatenate([kp_ref[...], kc_ref[...]], axis=0)
    v_all = jnp.concatenate([vp_ref[...], vc_ref[...]], axis=0)
    lane = lax.broadcasted_iota(jnp.int32, (1, GROUP_WIDTH), 1)
    key = lax.broadcasted_iota(jnp.int32, (1, 2 * Q_BLOCK), 1)
    first = jnp.where((key < Q_BLOCK) & first_tile, NEG, 0.0).astype(f32)
    heads = [(lane >= h * HEAD_DIM) & (lane < (h + 1) * HEAD_DIM) for h in range(HEADS_PER_GROUP)]

    def head_rows(a, h):
        return a[h * Q_BLOCK:(h + 1) * Q_BLOCK]

    def scores(j):
        q = q_ref[j * Q_BLOCK:(j + 1) * Q_BLOCK, :]
        k = k_all[j * Q_BLOCK:(j + 2) * Q_BLOCK]
        q_heads = jnp.concatenate([jnp.where(hd, q, jnp.zeros_like(q)) for hd in heads], axis=0)
        s = lax.dot_general(q_heads, k, (((1,), (1,)), ((), ())), preferred_element_type=f32)
        out = [head_rows(s, h) + bias_ref[h] for h in range(HEADS_PER_GROUP)]
        return [s_h + first for s_h in out] if j == 0 else out

    def finish(j, ss):
        v = v_all[j * Q_BLOCK:(j + 2) * Q_BLOCK]
        ms = [jnp.max(s, axis=-1, keepdims=True) for s in ss]
        ps = [jnp.exp(s - m) for s, m in zip(ss, ms)]
        ls = [jnp.sum(p, axis=-1, keepdims=True) for p in ps]
        pv = jnp.dot(jnp.concatenate([p.astype(bf16) for p in ps], axis=0), v, preferred_element_type=f32)
        o = jnp.zeros((Q_BLOCK, GROUP_WIDTH), f32)
        lse = jnp.zeros((Q_BLOCK, GROUP_WIDTH), f32)
        for h in range(HEADS_PER_GROUP):
            o = jnp.where(heads[h], head_rows(pv, h) / ls[h], o)
            lse = jnp.where(heads[h], ms[h] + jnp.log(ls[h]), lse)
        o_ref[j * Q_BLOCK:(j + 1) * Q_BLOCK, :] = o.astype(o_ref.dtype)
        lse_ref[j * Q_BLOCK:(j + 1) * Q_BLOCK, :] = lse

    n_blocks = ATTN_TILE // Q_BLOCK
    ss = scores(0)
    for j in range(n_blocks):
        ss_next = scores(j + 1) if j + 1 < n_blocks else None
        finish(j, ss)
        ss = ss_next


def _prompt_attn(q, k, v, bias, g):
    d, n, _ = q.shape
    q, k, v = (t.reshape(d * n, GROUP_WIDTH) for t in (q, k, v))
    per_tile = ATTN_TILE // Q_BLOCK
    cur = pl.BlockSpec((ATTN_TILE, GROUP_WIDTH), lambda t: (t, 0))
    prev = pl.BlockSpec((Q_BLOCK, GROUP_WIDTH), lambda t: (jnp.maximum(t * per_tile - 1, 0), 0))
    o, lse = pl.pallas_call(
        functools.partial(_prompt_attn_body, tiles_per_seq=n // ATTN_TILE),
        grid=(d * n // ATTN_TILE,),
        in_specs=[cur, prev, cur, prev, cur, _const_spec(bias.shape)],
        out_specs=(cur, cur),
        out_shape=(jax.ShapeDtypeStruct((d * n, GROUP_WIDTH), bf16),
                   jax.ShapeDtypeStruct((d * n, GROUP_WIDTH), f32)),
        compiler_params=pltpu.CompilerParams(dimension_semantics=("arbitrary",)),
        name=f"prompt_attn_{g}",
    )(q, k, k, v, v, bias)
    return o.reshape(d, n, GROUP_WIDTH), lse.reshape(d, n, GROUP_WIDTH)


def _prompt_out_body(o0, o1, o2, l0, l1, l2, gas, gm, mb, x, woa, wo, npost, y, so1, so2, sl1, sl2):
    tm = x.shape[0]
    for src, dst, d in ((o1, so1, GROUPS[1][1]), (l1, sl1, GROUPS[1][1]),
                        (o2, so2, GROUPS[2][1]), (l2, sl2, GROUPS[2][1])):
        for c in range(d):
            for j in range(GROUP_WIDTH // LANES):
                dst[j, pl.ds(c, tm // d, stride=d), :] = src[c, :, j * LANES:(j + 1) * LANES].astype(f32)
    rows = lambda s: jnp.concatenate([s[j] for j in range(GROUP_WIDTH // LANES)], axis=1)
    a = _combine((o0[0].astype(f32), rows(so1), rows(so2)), (l0[0], rows(sl1), rows(sl2)))
    y[...] = _tail(a, gas[...], gm[...], mb[...], x[...], woa[...], wo[...], npost[...])


def _prompt_out(os, ls, gas, gm, mb, x, w_out_a, w_o, norm_post):
    n = x.shape[0]
    tm = ROW_TILE
    row = lambda w: pl.BlockSpec((tm, w), lambda i: (i, 0))
    grp = lambda d: pl.BlockSpec((d, tm // d, GROUP_WIDTH), lambda i: (0, i, 0))
    gspecs = [grp(d) for _, d in GROUPS]
    return pl.pallas_call(
        _prompt_out_body,
        grid=(n // tm,),
        in_specs=gspecs + gspecs + [row(GROUP_WIDTH), row(D_MODEL), row(D_MODEL), row(D_MODEL),
                                    _const_spec(w_out_a.shape), _const_spec(w_o.shape),
                                    _const_spec(norm_post.shape)],
        out_specs=row(D_MODEL),
        out_shape=jax.ShapeDtypeStruct((n, D_MODEL), f32),
        scratch_shapes=[pltpu.VMEM((GROUP_WIDTH // LANES, tm, LANES), f32)] * 4,
        compiler_params=pltpu.CompilerParams(dimension_semantics=("arbitrary",),
                                             vmem_limit_bytes=VMEM_LIMIT),
        name="prompt_out",
    )(*os, *ls, gas, gm, mb, x, w_out_a, w_o, norm_post)


def _sample_in_body(x_ref, np_ref, w_ref, st_ref, cw_ref, cb_ref, lg_ref, lb_ref, wob_ref,
                    qt, kt, vt, gas, gm, mb, st_out):
    h = _rms(x_ref[...], np_ref[...]).astype(bf16)
    proj = jnp.dot(h, w_ref[...], preferred_element_type=f32)
    qt[...] = proj[:, 0:ATTN_WIDTH].T
    kt[...] = proj[:, ATTN_WIDTH:2 * ATTN_WIDTH].T
    vt[...] = proj[:, 2 * ATTN_WIDTH:C_QKV].T
    gas[...] = _silu(proj[:, C_QKV:C_GA])
    gm[...] = _sigmoid(proj[:, C_GB:C_MA])
    glu = proj[:, C_GA:C_GLU]
    u = glu[:, :CONV_WIDTH] * _sigmoid(glu[:, CONV_WIDTH:])
    hist = CONV_KERNEL - 1
    c = cb_ref[...] + cw_ref[hist:hist + 1, :] * u
    for t in range(hist):
        c = c + cw_ref[t:t + 1, :] * st_ref[t]
    for t in range(hist - 1):
        st_out[t] = st_ref[t + 1]
    st_out[hist - 1] = u
    mb[...] = _conv_gate(c, proj[:, C_GLU:C_GB], lg_ref[...], lb_ref[...], wob_ref[...],
                         proj[:, C_MA:C_MB])


def _sample_in(x, norm_pre, w_in, state, conv_w, conv_b, ln_g, ln_b, w_out_b):
    b = x.shape[0]
    t = jax.ShapeDtypeStruct((ATTN_WIDTH, b), f32)
    return pl.pallas_call(
        _sample_in_body,
        out_shape=(t, t, t,
                   jax.ShapeDtypeStruct((b, GROUP_WIDTH), f32),
                   jax.ShapeDtypeStruct((b, D_MODEL), f32),
                   jax.ShapeDtypeStruct((b, D_MODEL), f32),
                   jax.ShapeDtypeStruct(state.shape, f32)),
        compiler_params=pltpu.CompilerParams(vmem_limit_bytes=VMEM_LIMIT),
        name="sample_in",
    )(x, norm_pre, w_in, state, conv_w, conv_b, ln_g, ln_b, w_out_b)


def _sample_out_body(ot, lt, gas, gm, mb, x, woa, wo, npost, y):
    nb = x.shape[0]
    os = [ot[g * HEADS_PER_GROUP:(g + 1) * HEADS_PER_GROUP] for g in range(N_GROUPS)]
    ls = [lt[g * HEADS_PER_GROUP:(g + 1) * HEADS_PER_GROUP] for g in range(N_GROUPS)]
    a = _combine(os, ls).reshape(GROUP_WIDTH, nb).T
    y[...] = _tail(a, gas[...], gm[...], mb[...], x[...], woa[...], wo[...], npost[...])


def _sample_out(ot, lt, gas, gm, mb, x, w_out_a, w_o, norm_post):
    return pl.pallas_call(
        _sample_out_body,
        out_shape=jax.ShapeDtypeStruct(x.shape, f32),
        name="sample_out",
    )(ot, lt, gas, gm, mb, x, w_out_a, w_o, norm_post)


def _t5_buckets(dist):
    n = np.asarray(dist, dtype=np.int64)
    large = MAX_EXACT + (np.log(np.maximum(n, 1) / MAX_EXACT) / np.log(REL_MAX_DIST / MAX_EXACT)
                         * (N_BUCKETS - MAX_EXACT)).astype(np.int64)
    large = np.minimum(large, N_BUCKETS - 1)
    return np.where(n < MAX_EXACT, n, large).astype(np.int32)


def _bias_rows(rel_bias, g, buckets, valid):
    onehot = (jnp.asarray(buckets)[..., None] == jnp.arange(N_BUCKETS, dtype=jnp.int32)).astype(f32)
    cols = rel_bias[:, g * HEADS_PER_GROUP:(g + 1) * HEADS_PER_GROUP]
    tab = jnp.einsum('...b,bh->h...', onehot, cols, precision=lax.Precision.HIGHEST)
    return jnp.where(jnp.asarray(valid)[None], tab, NEG)


def _prompt_bias(rel_bias, g):
    _, dil = GROUPS[g]
    j = np.arange(Q_BLOCK)[:, None] + Q_BLOCK - np.arange(2 * Q_BLOCK)[None, :]
    valid = (j >= 0) & (j <= KEYS_PER_QUERY)
    return _bias_rows(rel_bias, g, _t5_buckets(dil * np.clip(j, 0, KEYS_PER_QUERY)), valid)


def _sample_bias(rel_bias, w_buf):
    dist = w_buf - np.arange(w_buf)
    rows = []
    for g, (window, dil) in enumerate(GROUPS):
        valid = (dist % dil == 0) & (dist <= window)
        rows.append(_bias_rows(rel_bias, g, _t5_buckets(dist), valid))
    bias = jnp.concatenate(rows, axis=0)[:, None, :]
    bias0 = jnp.broadcast_to(rel_bias[int(_t5_buckets(0)), :][:, None, None], (N_HEADS, 1, 128))
    return bias, bias0


def kernel(x_prompt, x_sample, cache_k, cache_v, state_conv, rel_bias, norm_pre, w_in, w_out_a, conv_w,
           conv_b, ln_g, ln_b, w_out_b, w_o, norm_post):
    assert x_prompt.shape[0] == 1 and cache_k.shape[0] == 1 and x_sample.shape[1] == 1
    n = x_prompt.shape[1]
    nb, w_buf = cache_k.shape[1], cache_k.shape[2]
    keep = min(GROUPS[-1][0], n)
    w_in_b, woa_b, wob_b, wo_b = (w[0].astype(bf16) for w in (w_in, w_out_a, w_out_b, w_o))
    cw, cb, lg, lb = conv_w[0], conv_b, ln_g, ln_b
    xp = x_prompt[0]

    xs = x_sample[:, 0, :]
    state = jnp.transpose(state_conv[0], (1, 0, 2))
    qt, kt, vt, gas_s, gm_s, mb_s, state_new = _sample_in(xs, norm_pre, w_in_b, state, cw, cb, lg, lb, wob_b)
    kc = jnp.transpose(cache_k[0], (0, 2, 3, 1))
    vc = jnp.transpose(cache_v[0], (0, 2, 3, 1))
    bias, bias0 = _sample_bias(rel_bias, w_buf)
    heads = lambda t: t.reshape(N_HEADS, HEAD_DIM, nb)

    (q0, k0, v0, q1, k1, v1, q2, k2, v2, klast, vlast, gas, gm, mb, ulast, ot, lt, ko, vo) = _prompt_in(
        xp, norm_pre, w_in_b, cw, cb, lg, lb, wob_b, keep, heads(qt), heads(kt), heads(vt), bias, bias0, kc, vc)
    os, ls = [], []
    for g, (q, k, v) in enumerate(((q0, k0, v0), (q1, k1, v1), (q2, k2, v2))):
        o, l = _prompt_attn(q, k, v, _prompt_bias(rel_bias, g), g)
        os.append(o)
        ls.append(l)
    y_prompt = _prompt_out(os, ls, gas, gm, mb, xp, woa_b, wo_b, norm_post)
    y_sample = _sample_out(ot, lt, gas_s, gm_s, mb_s, xs, woa_b, wo_b, norm_post)

    back = lambda t: jnp.transpose(t, (0, 3, 1, 2))[None]
    return (y_prompt[None], y_sample[:, None, :],
            klast.reshape(1, 1, keep, N_HEADS, HEAD_DIM), vlast.reshape(1, 1, keep, N_HEADS, HEAD_DIM),
            ulast[CARRY_ROWS - (CONV_KERNEL - 1):][None, None],
            back(ko), back(vo), jnp.transpose(state_new, (1, 0, 2))[None])
```

```python
import functools

import numpy as np
import jax
import jax.numpy as jnp
from jax import lax
from jax.experimental import pallas as pl
from jax.experimental.pallas import tpu as pltpu

D_MODEL = 1024
GROUPS = ((128, 1), (512, 4), (2048, 16))
N_GROUPS = len(GROUPS)
HEADS_PER_GROUP = 4
N_HEADS = N_GROUPS * HEADS_PER_GROUP
HEAD_DIM = 64
GROUP_WIDTH = HEADS_PER_GROUP * HEAD_DIM
ATTN_WIDTH = N_HEADS * HEAD_DIM
KEYS_PER_QUERY = 128
N_BUCKETS = 32
MAX_EXACT = 16
REL_MAX_DIST = 2048
CONV_WIDTH = D_MODEL // 2
CONV_KERNEL = 31
EPS = 1e-6
NEG = -1e30
SCALE = HEAD_DIM ** -0.5

C_QKV = 3 * ATTN_WIDTH
C_GA = C_QKV + GROUP_WIDTH
C_GLU = C_GA + 2 * CONV_WIDTH
C_GB = C_GLU + CONV_WIDTH
C_MA = C_GB + D_MODEL
C_MB = C_MA + D_MODEL

IN_TILE = 256
ROW_TILE = 1024
Q_BLOCK = 128
ATTN_TILE = 1024
CACHE_HEADS = 2
CACHE_SLOTS = 6
LANES = 128
SUBLANES = 8
CARRY_ROWS = 32
VMEM_LIMIT = 52 * 1024 * 1024

f32 = jnp.float32
bf16 = jnp.bfloat16


def _rms(x, g):
    return x * lax.rsqrt(jnp.mean(x * x, axis=-1, keepdims=True) + EPS) * g


def _sigmoid(x):
    return 1.0 / (1.0 + jnp.exp(-x))


def _silu(x):
    return x * _sigmoid(x)


def _layernorm(c, g, b):
    mu = jnp.mean(c, axis=-1, keepdims=True)
    var = jnp.mean(jnp.square(c - mu), axis=-1, keepdims=True)
    return (c - mu) * lax.rsqrt(var + EPS) * g + b


def _mm(a, w):
    return jnp.dot(a.astype(bf16), w, preferred_element_type=f32)


def _conv_gate(c, gb, lg, lb, wob, mbp):
    c = _silu(_layernorm(c, lg, lb)) * _silu(gb)
    return _sigmoid(mbp) * _mm(c, wob)


def _tail(a, gas, gm, mb, x, woa, wo, npost):
    y_a = _mm(a * gas, woa)
    merged = gm * y_a + mb
    return x + _rms(_mm(merged, wo), npost)


def _combine(os, ls):
    m = jnp.maximum(jnp.maximum(ls[0], ls[1]), ls[2])
    es = [jnp.exp(l - m) for l in ls]
    den = es[0] + es[1] + es[2]
    return (es[0] * os[0] + es[1] * os[1] + es[2] * os[2]) / den


def _cache_attend_shift(kc_ref, vc_ref, ko_ref, vo_ref, hs, seq, qt_ref, kt_ref, vt_ref, bias_ref, bias0_ref,
                        o_ref, lse_ref):
    hb, hd, w = kc_ref.shape
    window = GROUPS[hs.start // HEADS_PER_GROUP][0]
    lo = max(w - window, 0) // LANES * LANES
    sel = lax.broadcasted_iota(jnp.int32, (1, 1, qt_ref.shape[-1]), 2) == seq
    pick = lambda ref: jnp.sum(jnp.where(sel, ref[hs], 0.0), axis=2, keepdims=True)
    qc, kn, vn = pick(qt_ref), pick(kt_ref), pick(vt_ref)
    s8 = jnp.zeros((hb, SUBLANES, w - lo), f32)
    for r in range(0, hd, SUBLANES):
        s8 = s8 + kc_ref[:, r:r + SUBLANES, lo:] * qc[:, r:r + SUBLANES, :]
    s = jnp.sum(s8, axis=1, keepdims=True) * SCALE + bias_ref[hs][:, :, lo:]
    s_new = jnp.sum(kn * qc, axis=1, keepdims=True) * SCALE + bias0_ref[hs][:, :, 0:1]
    m = jnp.maximum(jnp.max(s, axis=2, keepdims=True), s_new)
    p = jnp.exp(s - m)
    p_new = jnp.exp(s_new - m)
    l = jnp.sum(p, axis=2, keepdims=True) + p_new
    acc = jnp.zeros((hb, hd, LANES), f32)
    for c in range(0, w - lo, LANES):
        acc = acc + vc_ref[:, :, lo + c:lo + c + LANES] * p[:, :, c:c + LANES]
    o = (jnp.sum(acc, axis=2, keepdims=True) + vn * p_new) / l
    o_ref[hs] = jnp.where(sel, o, o_ref[hs])
    lse_ref[hs] = jnp.where(sel, m + jnp.log(l), lse_ref[hs])
    newest = lax.broadcasted_iota(jnp.int32, (1, 1, w), 2) == w - 1
    ko_ref[...] = jnp.where(newest, kn, pltpu.roll(kc_ref[...], w - 1, axis=2))
    vo_ref[...] = jnp.where(newest, vn, pltpu.roll(vc_ref[...], w - 1, axis=2))


def _prompt_in_body(x_ref, np_ref, w_ref, cw_ref, cb_ref, lg_ref, lb_ref, wob_ref,
                    qt_ref, kt_ref, vt_ref, bias_ref, bias0_ref, kc_hbm, vc_hbm,
                    q0, k0, v0, q1, k1, v1, q2, k2, v2, klast, vlast, gas, gm, mb, ulast,
                    ot_ref, lt_ref, ko_hbm, vo_hbm,
                    qkv_s, ubuf, cacc, kbuf, vbuf, kobuf, vobuf, sem_in, sem_out, *, per_tile):
    i = pl.program_id(0)
    nt = pl.num_programs(0)
    tm = x_ref.shape[0]
    slots, hb = kbuf.shape[0], kbuf.shape[1]
    per_seq = N_HEADS // hb
    ahead = slots - 1

    def block(t, j):
        return t * (per_tile // per_seq) + j // per_seq, (j % per_seq) * hb, j % slots

    def in_copies(t, j):
        seq, h0, slot = block(t, j)
        return (pltpu.make_async_copy(kc_hbm.at[seq, pl.ds(h0, hb)], kbuf.at[slot], sem_in.at[slot, 0]),
                pltpu.make_async_copy(vc_hbm.at[seq, pl.ds(h0, hb)], vbuf.at[slot], sem_in.at[slot, 1]))

    def out_copies(t, j):
        seq, h0, slot = block(t, j)
        return (pltpu.make_async_copy(kobuf.at[slot], ko_hbm.at[seq, pl.ds(h0, hb)], sem_out.at[slot, 0]),
                pltpu.make_async_copy(vobuf.at[slot], vo_hbm.at[seq, pl.ds(h0, hb)], sem_out.at[slot, 1]))

    def start(copies):
        for cp in copies:
            cp.start()

    def wait(copies):
        for cp in copies:
            cp.wait()

    fetch = write_back = start

    @pl.when(i == 0)
    def _():
        ot_ref[...] = jnp.zeros(ot_ref.shape, f32)
        lt_ref[...] = jnp.zeros(lt_ref.shape, f32)
        ubuf[0:CARRY_ROWS, :] = jnp.zeros((CARRY_ROWS, CONV_WIDTH), f32)
        for j in range(ahead):
            fetch(in_copies(i, j))

    def cache_block(j):
        seq, h0, slot = block(i, j)
        wait(in_copies(i, j))
        if j + ahead < per_tile:
            fetch(in_copies(i, j + ahead))
        else:
            @pl.when(i + 1 < nt)
            def _():
                fetch(in_copies(i + 1, j + ahead - per_tile))
        if j >= slots:
            wait(out_copies(i, j - slots))
        else:
            @pl.when(i > 0)
            def _():
                wait(out_copies(i - 1, per_tile - slots + j))
        _cache_attend_shift(kbuf.at[slot], vbuf.at[slot], kobuf.at[slot], vobuf.at[slot], slice(h0, h0 + hb), seq,
                            qt_ref, kt_ref, vt_ref, bias_ref, bias0_ref, ot_ref, lt_ref)
        if j < len(work):
            work[j]()
        write_back(out_copies(i, j))

    h = _rms(x_ref[...], np_ref[...]).astype(bf16)

    def project_qkv():
        qkv = jnp.dot(h, w_ref[:, 0:C_QKV], preferred_element_type=f32)
        klast[...] = qkv[:, ATTN_WIDTH:2 * ATTN_WIDTH]
        vlast[...] = qkv[:, 2 * ATTN_WIDTH:3 * ATTN_WIDTH]
        for j in range(C_QKV // LANES):
            qkv_s[j] = qkv[:, j * LANES:(j + 1) * LANES]

    def regroup_qkv(groups):
        outs = ((q0, k0, v0), (q1, k1, v1), (q2, k2, v2))
        for g in groups:
            dil = GROUPS[g][1]
            for part in range(3):
                col = part * ATTN_WIDTH + g * GROUP_WIDTH
                for c in range(dil):
                    for j in range(GROUP_WIDTH // LANES):
                        rows = qkv_s[col // LANES + j, pl.ds(c, tm // dil, stride=dil), :]
                        if part == 0:
                            rows = rows * SCALE
                        outs[g][part][c, :, j * LANES:(j + 1) * LANES] = rows.astype(bf16)

    def gates():
        gas[...] = _silu(jnp.dot(h, w_ref[:, C_QKV:C_GA], preferred_element_type=f32)).astype(gas.dtype)
        gm[...] = _sigmoid(jnp.dot(h, w_ref[:, C_GB:C_MA], preferred_element_type=f32)).astype(gm.dtype)

    def glu_unit():
        glu = jnp.dot(h, w_ref[:, C_GA:C_GLU], preferred_element_type=f32)
        ubuf[CARRY_ROWS:CARRY_ROWS + tm, :] = glu[:, :CONV_WIDTH] * _sigmoid(glu[:, CONV_WIDTH:])
        cacc[...] = jnp.zeros((tm, CONV_WIDTH), f32) + cb_ref[...]

    off = CARRY_ROWS - (CONV_KERNEL - 1)

    def conv_taps(shifts):
        c = cacc[...]
        for s in shifts:
            taps = [t for t in range(CONV_KERNEL) if (off + t) % SUBLANES == s]
            span = max(off + t - s for t in taps) + tm
            shifted = ubuf[s:s + span, :]
            for t in taps:
                c = c + cw_ref[t:t + 1, :] * shifted[off + t - s:off + t - s + tm]
        cacc[...] = c

    def conv_gate():
        last = ubuf[tm:tm + CARRY_ROWS, :]
        ubuf[0:CARRY_ROWS, :] = last
        ulast[...] = last
        gb = jnp.dot(h, w_ref[:, C_GLU:C_GB], preferred_element_type=f32)
        mbp = jnp.dot(h, w_ref[:, C_MA:C_MB], preferred_element_type=f32)
        mb[...] = _conv_gate(cacc[...], gb, lg_ref[...], lb_ref[...], wob_ref[...], mbp).astype(mb.dtype)

    work = [project_qkv, functools.partial(regroup_qkv, (0, 1)), functools.partial(regroup_qkv, (2,)),
            gates, glu_unit]
    work += [functools.partial(conv_taps, range(s, s + 2)) for s in range(0, SUBLANES, 2)]
    work += [conv_gate]
    assert len(work) <= per_tile
    for j in range(per_tile):
        cache_block(j)

    @pl.when(i == nt - 1)
    def _():
        for j in range(per_tile - slots, per_tile):
            wait(out_copies(i, j))


def _const_spec(shape):
    nd = len(shape)
    return pl.BlockSpec(shape, lambda *_: (0,) * nd, pipeline_mode=pl.Buffered(1))


def _prompt_in(x, norm_pre, w_in, conv_w, conv_b, ln_g, ln_b, w_out_b, keep, qt, kt, vt, bias, bias0, kc, vc):
    n = x.shape[0]
    tm = IN_TILE
    nt = n // tm
    keep_tiles = keep // tm
    nb, nh, hd, w = kc.shape
    per_seq = nh // CACHE_HEADS
    per_tile = nb * per_seq // nt
    assert per_tile * nt == nb * per_seq and per_tile % per_seq == 0 and per_tile % CACHE_SLOTS == 0
    row = lambda w: pl.BlockSpec((tm, w), lambda i: (i, 0))
    grp = lambda d: pl.BlockSpec((d, tm // d, GROUP_WIDTH), lambda i: (0, i, 0))
    last = pl.BlockSpec((tm, ATTN_WIDTH), lambda i: (jnp.maximum(i - (nt - keep_tiles), 0), 0))
    hbm = pl.BlockSpec(memory_space=pl.ANY)
    out_shape, out_specs = [], []
    for _, d in GROUPS:
        for _ in range(3):
            out_shape.append(jax.ShapeDtypeStruct((d, n // d, GROUP_WIDTH), bf16))
            out_specs.append(grp(d))
    out_shape += [jax.ShapeDtypeStruct((keep, ATTN_WIDTH), f32)] * 2
    out_specs += [last, last]
    out_shape += [jax.ShapeDtypeStruct((n, GROUP_WIDTH), bf16),
                  jax.ShapeDtypeStruct((n, D_MODEL), bf16),
                  jax.ShapeDtypeStruct((n, D_MODEL), bf16),
                  jax.ShapeDtypeStruct((CARRY_ROWS, CONV_WIDTH), f32)]
    out_specs += [row(GROUP_WIDTH), row(D_MODEL), row(D_MODEL),
                  pl.BlockSpec((CARRY_ROWS, CONV_WIDTH), lambda i: (0, 0))]
    out_shape += [jax.ShapeDtypeStruct((nh, hd, nb), f32), jax.ShapeDtypeStruct((nh, 1, nb), f32),
                  jax.ShapeDtypeStruct(kc.shape, f32), jax.ShapeDtypeStruct(vc.shape, f32)]
    out_specs += [pl.BlockSpec((nh, hd, nb), lambda i: (0, 0, 0)), pl.BlockSpec((nh, 1, nb), lambda i: (0, 0, 0)),
                  hbm, hbm]
    cache_buf = pltpu.VMEM((CACHE_SLOTS, CACHE_HEADS, hd, w), f32)
    sems = pltpu.SemaphoreType.DMA((CACHE_SLOTS, 2))
    return pl.pallas_call(
        functools.partial(_prompt_in_body, per_tile=per_tile),
        grid=(nt,),
        in_specs=[row(D_MODEL), _const_spec(norm_pre.shape), _const_spec(w_in.shape),
                  _const_spec(conv_w.shape), _const_spec(conv_b.shape), _const_spec(ln_g.shape),
                  _const_spec(ln_b.shape), _const_spec(w_out_b.shape),
                  _const_spec(qt.shape), _const_spec(kt.shape), _const_spec(vt.shape),
                  _const_spec(bias.shape), _const_spec(bias0.shape), hbm, hbm],
        out_specs=out_specs,
        out_shape=out_shape,
        scratch_shapes=[pltpu.VMEM((C_QKV // LANES, tm, LANES), f32),
                        pltpu.VMEM((tm + CARRY_ROWS, CONV_WIDTH), f32),
                        pltpu.VMEM((tm, CONV_WIDTH), f32),
                        cache_buf, cache_buf, cache_buf, cache_buf, sems, sems],
        compiler_params=pltpu.CompilerParams(dimension_semantics=("arbitrary",),
                                             vmem_limit_bytes=VMEM_LIMIT),
        name="prompt_in",
    )(x, norm_pre, w_in, conv_w, conv_b, ln_g, ln_b, w_out_b, qt, kt, vt, bias, bias0, kc, vc)


def _attend_tile(q_ref, kp_ref, kc_ref, vp_ref, vc_ref, bias_ref, o_ref, lse_ref, first_tile):
    k_all = jnp.concatenate([kp_ref[...], kc_ref[...]], axis=0)
    v_all = jnp.concatenate([vp_ref[...], vc_ref[...]], axis=0)
    lane = lax.broadcasted_iota(jnp.int32, (1, GROUP_WIDTH), 1)
    key = lax.broadcasted_iota(jnp.int32, (1, 2 * Q_BLOCK), 1)
    first = jnp.where((key < Q_BLOCK) & first_tile, NEG, 0.0).astype(f32)
    heads = [(lane >= h * HEAD_DIM) & (lane < (h + 1) * HEAD_DIM) for h in range(HEADS_PER_GROUP)]

    def head_rows(a, h):
        return a[h * Q_BLOCK:(h + 1) * Q_BLOCK]

    def scores(j):
        q = q_ref[j * Q_BLOCK:(j + 1) * Q_BLOCK, :]
        k = k_all[j * Q_BLOCK:(j + 2) * Q_BLOCK]
        q_heads = jnp.concatenate([jnp.where(hd, q, jnp.zeros_like(q)) for hd in heads], axis=0)
        s = lax.dot_general(q_heads, k, (((1,), (1,)), ((), ())), preferred_element_type=f32)
        out = [head_rows(s, h) + bias_ref[h] for h in range(HEADS_PER_GROUP)]
        return [s_h + first for s_h in out] if j == 0 else out

    def finish(j, ss):
        v = v_all[j * Q_BLOCK:(j + 2) * Q_BLOCK]
        ms = [jnp.max(s, axis=-1, keepdims=True) for s in ss]
        ps = [jnp.exp(s - m) for s, m in zip(ss, ms)]
        ls = [jnp.sum(p, axis=-1, keepdims=True) for p in ps]
        pv = jnp.dot(jnp.concatenate([p.astype(bf16) for p in ps], axis=0), v, preferred_element_type=f32)
        o = head_rows(pv, 0) / ls[0]
        lse = jnp.broadcast_to(ms[0] + jnp.log(ls[0]), (Q_BLOCK, GROUP_WIDTH))
        for h in range(1, HEADS_PER_GROUP):
            o = jnp.where(heads[h], head_rows(pv, h) / ls[h], o)
            lse = jnp.where(heads[h], ms[h] + jnp.log(ls[h]), lse)
        o_ref[j * Q_BLOCK:(j + 1) * Q_BLOCK, :] = o.astype(o_ref.dtype)
        lse_ref[j * Q_BLOCK:(j + 1) * Q_BLOCK, :] = lse

    n_blocks = ATTN_TILE // Q_BLOCK
    ss = scores(0)
    for j in range(n_blocks):
        ss_next = scores(j + 1) if j + 1 < n_blocks else None
        finish(j, ss)
        ss = ss_next


def _prompt_attn_body(*refs, tiles_per_seq):
    n_groups = len(tiles_per_seq)
    ins, outs = refs[:6 * n_groups], refs[6 * n_groups:]
    for gi in range(n_groups):
        @pl.when(pl.program_id(0) == gi)
        def _():
            _attend_tile(*ins[6 * gi:6 * gi + 6], *outs[2 * gi:2 * gi + 2],
                         pl.program_id(1) % tiles_per_seq[gi] == 0)


def _attn_specs():
    cur = pl.BlockSpec((ATTN_TILE, GROUP_WIDTH), lambda t: (t, 0))
    prev = pl.BlockSpec((Q_BLOCK, GROUP_WIDTH), lambda t: (jnp.maximum(t * (ATTN_TILE // Q_BLOCK) - 1, 0), 0))
    return cur, prev


def _prompt_attn(qkvs, biases):
    n_groups = len(qkvs)
    shapes = [q.shape for q, _, _ in qkvs]
    rows = shapes[0][0] * shapes[0][1]
    assert all(d * n == rows for d, n, _ in shapes)
    tiles = rows // ATTN_TILE
    per_tile = ATTN_TILE // Q_BLOCK

    def specs(gi):
        tile = lambda g, t: jnp.where(g == gi, t, jnp.where(g < gi, 0, tiles - 1))
        cur = pl.BlockSpec((ATTN_TILE, GROUP_WIDTH), lambda g, t: (tile(g, t), 0))
        prev = pl.BlockSpec((Q_BLOCK, GROUP_WIDTH), lambda g, t: (jnp.maximum(tile(g, t) * per_tile - 1, 0), 0))
        return cur, prev

    args, in_specs, out_specs, out_shape = [], [], [], []
    for gi, ((q, k, v), bias) in enumerate(zip(qkvs, biases)):
        cur, prev = specs(gi)
        q, k, v = (t.reshape(rows, GROUP_WIDTH) for t in (q, k, v))
        args += [q, k, k, v, v, bias]
        in_specs += [cur, prev, cur, prev, cur, _const_spec(bias.shape)]
        out_specs += [cur, cur]
        out_shape += [jax.ShapeDtypeStruct((rows, GROUP_WIDTH), bf16), jax.ShapeDtypeStruct((rows, GROUP_WIDTH), f32)]
    res = pl.pallas_call(
        functools.partial(_prompt_attn_body, tiles_per_seq=tuple(n // ATTN_TILE for _, n, _ in shapes)),
        grid=(n_groups, tiles),
        in_specs=in_specs,
        out_specs=out_specs,
        out_shape=out_shape,
        compiler_params=pltpu.CompilerParams(dimension_semantics=("arbitrary", "arbitrary")),
        name="prompt_attn",
    )(*args)
    os = [res[2 * gi].reshape(shapes[gi]) for gi in range(n_groups)]
    ls = [res[2 * gi + 1].reshape(shapes[gi]) for gi in range(n_groups)]
    return os, ls


def _prompt_out_body(q0, k0p, k0c, v0p, v0c, bias0, o1, o2, l1, l2, gas, gm, mb, x, woa, wo, npost, y,
                     so0, sl0, so1, so2, sl1, sl2):
    tm = x.shape[0]
    _attend_tile(q0, k0p, k0c, v0p, v0c, bias0, so0, sl0, pl.program_id(0) == 0)
    for src, dst, d in ((o1, so1, GROUPS[1][1]), (l1, sl1, GROUPS[1][1]),
                        (o2, so2, GROUPS[2][1]), (l2, sl2, GROUPS[2][1])):
        for c in range(d):
            for j in range(GROUP_WIDTH // LANES):
                dst[j, pl.ds(c, tm // d, stride=d), :] = src[c, :, j * LANES:(j + 1) * LANES].astype(f32)
    rows = lambda s: jnp.concatenate([s[j] for j in range(GROUP_WIDTH // LANES)], axis=1)
    a = _combine((so0[...], rows(so1), rows(so2)), (sl0[...], rows(sl1), rows(sl2)))
    y[...] = _tail(a, gas[...], gm[...], mb[...], x[...], woa[...], wo[...], npost[...])


def _prompt_out(qkv0, bias0, os, ls, gas, gm, mb, x, w_out_a, w_o, norm_post):
    n = x.shape[0]
    tm = ROW_TILE
    assert tm == ATTN_TILE and GROUPS[0][1] == 1
    q0, k0, v0 = (t.reshape(n, GROUP_WIDTH) for t in qkv0)
    cur, prev = _attn_specs()
    row = lambda w: pl.BlockSpec((tm, w), lambda i: (i, 0))
    grp = lambda d: pl.BlockSpec((d, tm // d, GROUP_WIDTH), lambda i: (0, i, 0))
    gspecs = [grp(d) for _, d in GROUPS[1:]]
    flat = pltpu.VMEM((tm, GROUP_WIDTH), f32)
    split = pltpu.VMEM((GROUP_WIDTH // LANES, tm, LANES), f32)
    return pl.pallas_call(
        _prompt_out_body,
        grid=(n // tm,),
        in_specs=[cur, prev, cur, prev, cur, _const_spec(bias0.shape)] + gspecs + gspecs
                 + [row(GROUP_WIDTH), row(D_MODEL), row(D_MODEL), row(D_MODEL),
                    _const_spec(w_out_a.shape), _const_spec(w_o.shape), _const_spec(norm_post.shape)],
        out_specs=row(D_MODEL),
        out_shape=jax.ShapeDtypeStruct((n, D_MODEL), f32),
        scratch_shapes=[flat, flat, split, split, split, split],
        compiler_params=pltpu.CompilerParams(dimension_semantics=("arbitrary",),
                                             vmem_limit_bytes=VMEM_LIMIT),
        name="prompt_out",
    )(q0, k0, k0, v0, v0, bias0, *os, *ls, gas, gm, mb, x, w_out_a, w_o, norm_post)


def _sample_in_body(x_ref, np_ref, w_ref, st_ref, cw_ref, cb_ref, lg_ref, lb_ref, wob_ref,
                    qt, kt, vt, gas, gm, mb, st_out):
    h = _rms(x_ref[...], np_ref[...]).astype(bf16)
    proj = jnp.dot(h, w_ref[...], preferred_element_type=f32)
    qt[...] = proj[:, 0:ATTN_WIDTH].T
    kt[...] = proj[:, ATTN_WIDTH:2 * ATTN_WIDTH].T
    vt[...] = proj[:, 2 * ATTN_WIDTH:C_QKV].T
    gas[...] = _silu(proj[:, C_QKV:C_GA])
    gm[...] = _sigmoid(proj[:, C_GB:C_MA])
    glu = proj[:, C_GA:C_GLU]
    u = glu[:, :CONV_WIDTH] * _sigmoid(glu[:, CONV_WIDTH:])
    hist = CONV_KERNEL - 1
    c = cb_ref[...] + cw_ref[hist:hist + 1, :] * u
    for t in range(hist):
        c = c + cw_ref[t:t + 1, :] * st_ref[t]
    for t in range(hist - 1):
        st_out[t] = st_ref[t + 1]
    st_out[hist - 1] = u
    mb[...] = _conv_gate(c, proj[:, C_GLU:C_GB], lg_ref[...], lb_ref[...], wob_ref[...],
                         proj[:, C_MA:C_MB])


def _sample_in(x, norm_pre, w_in, state, conv_w, conv_b, ln_g, ln_b, w_out_b):
    b = x.shape[0]
    t = jax.ShapeDtypeStruct((ATTN_WIDTH, b), f32)
    return pl.pallas_call(
        _sample_in_body,
        out_shape=(t, t, t,
                   jax.ShapeDtypeStruct((b, GROUP_WIDTH), f32),
                   jax.ShapeDtypeStruct((b, D_MODEL), f32),
                   jax.ShapeDtypeStruct((b, D_MODEL), f32),
                   jax.ShapeDtypeStruct(state.shape, f32)),
        compiler_params=pltpu.CompilerParams(vmem_limit_bytes=VMEM_LIMIT),
        name="sample_in",
    )(x, norm_pre, w_in, state, conv_w, conv_b, ln_g, ln_b, w_out_b)


def _sample_out_body(ot, lt, gas, gm, mb, x, woa, wo, npost, y):
    nb = x.shape[0]
    os = [ot[g * HEADS_PER_GROUP:(g + 1) * HEADS_PER_GROUP] for g in range(N_GROUPS)]
    ls = [lt[g * HEADS_PER_GROUP:(g + 1) * HEADS_PER_GROUP] for g in range(N_GROUPS)]
    a = _combine(os, ls).reshape(GROUP_WIDTH, nb).T
    y[...] = _tail(a, gas[...], gm[...], mb[...], x[...], woa[...], wo[...], npost[...])


def _sample_out(ot, lt, gas, gm, mb, x, w_out_a, w_o, norm_post):
    return pl.pallas_call(
        _sample_out_body,
        out_shape=jax.ShapeDtypeStruct(x.shape, f32),
        name="sample_out",
    )(ot, lt, gas, gm, mb, x, w_out_a, w_o, norm_post)


def _t5_buckets(dist):
    n = np.asarray(dist, dtype=np.int64)
    large = MAX_EXACT + (np.log(np.maximum(n, 1) / MAX_EXACT) / np.log(REL_MAX_DIST / MAX_EXACT)
                         * (N_BUCKETS - MAX_EXACT)).astype(np.int64)
    large = np.minimum(large, N_BUCKETS - 1)
    return np.where(n < MAX_EXACT, n, large).astype(np.int32)


def _bias_rows(rel_bias, g, buckets, valid):
    onehot = (jnp.asarray(buckets)[..., None] == jnp.arange(N_BUCKETS, dtype=jnp.int32)).astype(f32)
    cols = rel_bias[:, g * HEADS_PER_GROUP:(g + 1) * HEADS_PER_GROUP]
    tab = jnp.einsum('...b,bh->h...', onehot, cols, precision=lax.Precision.HIGHEST)
    return jnp.where(jnp.asarray(valid)[None], tab, NEG)


def _prompt_bias(rel_bias, g):
    _, dil = GROUPS[g]
    j = np.arange(Q_BLOCK)[:, None] + Q_BLOCK - np.arange(2 * Q_BLOCK)[None, :]
    valid = (j >= 0) & (j <= KEYS_PER_QUERY)
    return _bias_rows(rel_bias, g, _t5_buckets(dil * np.clip(j, 0, KEYS_PER_QUERY)), valid)


def _sample_bias(rel_bias, w_buf):
    dist = w_buf - np.arange(w_buf)
    rows = []
    for g, (window, dil) in enumerate(GROUPS):
        valid = (dist % dil == 0) & (dist <= window)
        rows.append(_bias_rows(rel_bias, g, _t5_buckets(dist), valid))
    bias = jnp.concatenate(rows, axis=0)[:, None, :]
    bias0 = jnp.broadcast_to(rel_bias[int(_t5_buckets(0)), :][:, None, None], (N_HEADS, 1, 128))
    return bias, bias0


def kernel(x_prompt, x_sample, cache_k, cache_v, state_conv, rel_bias, norm_pre, w_in, w_out_a, conv_w,
           conv_b, ln_g, ln_b, w_out_b, w_o, norm_post):
    assert x_prompt.shape[0] == 1 and cache_k.shape[0] == 1 and x_sample.shape[1] == 1
    n = x_prompt.shape[1]
    nb, w_buf = cache_k.shape[1], cache_k.shape[2]
    keep = min(GROUPS[-1][0], n)
    w_in_b, woa_b, wob_b, wo_b = (w[0].astype(bf16) for w in (w_in, w_out_a, w_out_b, w_o))
    cw, cb, lg, lb = conv_w[0], conv_b, ln_g, ln_b
    xp = x_prompt[0]

    xs = x_sample[:, 0, :]
    state = jnp.transpose(state_conv[0], (1, 0, 2))
    qt, kt, vt, gas_s, gm_s, mb_s, state_new = _sample_in(xs, norm_pre, w_in_b, state, cw, cb, lg, lb, wob_b)
    kc = jnp.transpose(cache_k[0], (0, 2, 3, 1))
    vc = jnp.transpose(cache_v[0], (0, 2, 3, 1))
    bias, bias0 = _sample_bias(rel_bias, w_buf)
    heads = lambda t: t.reshape(N_HEADS, HEAD_DIM, nb)

    (q0, k0, v0, q1, k1, v1, q2, k2, v2, klast, vlast, gas, gm, mb, ulast, ot, lt, ko, vo) = _prompt_in(
        xp, norm_pre, w_in_b, cw, cb, lg, lb, wob_b, keep, heads(qt), heads(kt), heads(vt), bias, bias0, kc, vc)
    os, ls = _prompt_attn(((q1, k1, v1), (q2, k2, v2)), [_prompt_bias(rel_bias, g) for g in (1, 2)])
    y_prompt = _prompt_out((q0, k0, v0), _prompt_bias(rel_bias, 0), os, ls, gas, gm, mb, xp, woa_b, wo_b,
                           norm_post)
    y_sample = _sample_out(ot, lt, gas_s, gm_s, mb_s, xs, woa_b, wo_b, norm_post)

    back = lambda t: jnp.transpose(t, (0, 3, 1, 2))[None]
    return (y_prompt[None], y_sample[:, None, :],
            klast.reshape(1, 1, keep, N_HEADS, HEAD_DIM), vlast.reshape(1, 1, keep, N_HEADS, HEAD_DIM),
            ulast[CARRY_ROWS - (CONV_KERNEL - 1):][None, None],
            back(ko), back(vo), jnp.transpose(state_new, (1, 0, 2))[None])
```
